```python
import math
import jax, jax.numpy as jnp
from jax import lax
import numpy as np

D_MODEL = 1024
BATCH = 2
SEQ = 8192
DEPTH = 4

GRID_W = 64
HEAD_DIM = 64
BLOCK = 128
NA_HEADS = 8
NA_WIN_ROWS = 8
NA_WIN_COLS = 16
DIFF_HEADS = 4
SWA_Q_HEADS = 16
SWA_KV_HEADS = 4
SWA_WINDOW = 128
PEER_HEADS = 8
PEER_N_KEYS = 128
PEER_N_EXPERTS = PEER_N_KEYS * PEER_N_KEYS
PEER_KEY_DIM = 256
PEER_TOPK = 16

ROPE_THETA = 10000.0
LN_EPS = 1e-5
NEG = -1e30
DEEPNORM_ALPHA = (2 * DEPTH) ** 0.25
DEEPNORM_BETA = (8 * DEPTH) ** -0.25
N_EVEN = (DEPTH + 1) // 2
N_ODD = DEPTH // 2

A_WIDTH = NA_HEADS * HEAD_DIM
B_QK_WIDTH = DIFF_HEADS * 2 * HEAD_DIM
B_V_WIDTH = DIFF_HEADS * 2 * HEAD_DIM
EVEN_IN = 3 * A_WIDTH + 2 * B_QK_WIDTH + B_V_WIDTH
C_Q = SWA_Q_HEADS * HEAD_DIM
C_KV = SWA_KV_HEADS * HEAD_DIM
ODD_IN = C_Q + 2 * C_KV

kernel_name = "hybrid_natten_diff_swa_peer_encoder"


def layer_norm(x, g, b):
    xf = x.astype(jnp.float32)
    mu = jnp.mean(xf, -1, keepdims=True)
    var = jnp.mean(jnp.square(xf - mu), -1, keepdims=True)
    return ((xf - mu) * lax.rsqrt(var + LN_EPS) * g.astype(jnp.float32) + b.astype(jnp.float32)).astype(x.dtype)


def rope_tables(seq):
    inv = 1.0 / (ROPE_THETA ** (jnp.arange(0, HEAD_DIM, 2, dtype=jnp.float32) / HEAD_DIM))
    ang = jnp.arange(seq, dtype=jnp.float32)[:, None] * inv[None, :]
    return jnp.cos(ang), jnp.sin(ang)


def apply_rope(t, cos, sin):
    shp = (1, t.shape[1]) + (1,) * (t.ndim - 3) + (HEAD_DIM // 2,)
    cs, sn = cos.reshape(shp), sin.reshape(shp)
    t1, t2 = jnp.split(t.astype(jnp.float32), 2, -1)
    return jnp.concatenate([t1 * cs - t2 * sn, t1 * sn + t2 * cs], -1).astype(t.dtype)


def neighbourhood_attention(q, k, v, rpb):
    B, S = q.shape[:2]
    rows = S // GRID_W
    wr = min(NA_WIN_ROWS, rows)
    to_grid = lambda t: t.reshape(B, rows, GRID_W, NA_HEADS, HEAD_DIM)
    qg, kg, vg = to_grid(q), to_grid(k), to_grid(v)
    r = jnp.arange(rows)
    row_start = jnp.clip(r - wr // 2, 0, rows - wr)
    key_rows = row_start[:, None] + jnp.arange(wr)[None, :]
    kb = kg[:, key_rows]
    vb = vg[:, key_rows]
    cidx = jnp.arange(GRID_W)
    col_start = jnp.clip(cidx - NA_WIN_COLS // 2, 0, GRID_W - NA_WIN_COLS)
    col_ok = (cidx[None, :] >= col_start[:, None]) & (cidx[None, :] < col_start[:, None] + NA_WIN_COLS)
    dr = key_rows - r[:, None]
    dc = cidx[None, :] - cidx[:, None]
    bias = rpb[:, dr + NA_WIN_ROWS - 1]
    bias = jnp.take(bias, jnp.clip(dc + NA_WIN_COLS - 1, 0, 2 * NA_WIN_COLS - 2), axis=-1)
    bias = bias.transpose(0, 1, 3, 2, 4).astype(jnp.float32)
    s = jnp.einsum('brqhd,brwkhd->bhrqwk', qg, kb).astype(jnp.float32) * (HEAD_DIM ** -0.5)
    s = jnp.where(col_ok[None, None, None, :, None, :], s + bias[None], NEG)
    p = jax.nn.softmax(s.reshape(s.shape[:4] + (wr * GRID_W,)), -1).reshape(s.shape).astype(v.dtype)
    o = jnp.einsum('bhrqwk,brwkhd->brqhd', p, vb)
    return o.reshape(B, S, A_WIDTH)


def diff_attention(q, k, v, lam, sub_g, lambda_init):
    B, S = q.shape[:2]
    nb = S // BLOCK
    qb = q.reshape(B, nb, BLOCK, DIFF_HEADS, 2, HEAD_DIM).transpose(1, 0, 2, 3, 4, 5)

    def one_block(qblk):
        s = jnp.einsum('bqhmd,bkhmd->bhmqk', qblk, k).astype(jnp.float32) * (HEAD_DIM ** -0.5)
        p = jax.nn.softmax(s, -1)
        w = (p[:, :, 0] - lam * p[:, :, 1]).astype(v.dtype)
        return jnp.einsum('bhqk,bkhe->bqhe', w, v)

    o = lax.map(one_block, qb)
    o = o.transpose(1, 0, 2, 3, 4).reshape(B, S, DIFF_HEADS, 2 * HEAD_DIM).astype(jnp.float32)
    o = o * lax.rsqrt(jnp.mean(o * o, -1, keepdims=True) + LN_EPS) * sub_g.astype(jnp.float32)
    return (o * (1.0 - lambda_init)).astype(q.dtype).reshape(B, S, B_V_WIDTH)


def sliding_window_gqa(q, k, v, sink):
    B, S = q.shape[:2]
    nb = S // BLOCK
    G = SWA_Q_HEADS // SWA_KV_HEADS
    qb = q.reshape(B, nb, BLOCK, SWA_KV_HEADS, G, HEAD_DIM)
    pad = lambda t: jnp.pad(t, ((0, 0), (BLOCK, BLOCK), (0, 0), (0, 0))).reshape(B, nb + 2, BLOCK, SWA_KV_HEADS, HEAD_DIM)
    band = lambda t: jnp.concatenate([t[:, i:i + nb] for i in range(3)], axis=2)
    kb, vb = band(pad(k)), band(pad(v))
    s = jnp.einsum('bnqhgd,bnkhd->bnhgqk', qb, kb).astype(jnp.float32) * (HEAD_DIM ** -0.5)
    blk = jnp.arange(nb)[:, None, None]
    qpos = blk * BLOCK + jnp.arange(BLOCK)[None, :, None]
    kpos = (blk - 1) * BLOCK + jnp.arange(3 * BLOCK)[None, None, :]
    ok = (jnp.abs(qpos - kpos) <= SWA_WINDOW) & (kpos >= 0) & (kpos < S)
    s = jnp.where(ok[None, :, None, None], s, NEG)
    sink_l = sink.astype(jnp.float32).reshape(1, 1, SWA_KV_HEADS, G, 1, 1)
    m = jnp.maximum(jnp.max(s, -1, keepdims=True), sink_l)
    e = jnp.exp(s - m)
    p = e / (jnp.sum(e, -1, keepdims=True) + jnp.exp(sink_l - m))
    o = jnp.einsum('bnhgqk,bnkhd->bnqhgd', p.astype(v.dtype), vb)
    return o.reshape(B, S, C_Q)


def peer(h, w_query, sub_keys, expert_u, expert_v):
    B, S, D = h.shape
    nb = S // BLOCK
    half = PEER_KEY_DIM // 2
    hb = h.reshape(B, nb, BLOCK, D).transpose(1, 0, 2, 3)

    def one_block(xb):
        q = jnp.einsum('btd,de->bte', xb, w_query).reshape(B, BLOCK, PEER_HEADS, 2, half)
        s = jnp.einsum('bthpe,hpne->bthpn', q, sub_keys).astype(jnp.float32)
        sv, si = lax.top_k(s, PEER_TOPK)
        cand = sv[..., 0, :, None] + sv[..., 1, None, :]
        cidx = si[..., 0, :, None] * PEER_N_KEYS + si[..., 1, None, :]
        fv, fpos = lax.top_k(cand.reshape(B, BLOCK, PEER_HEADS, PEER_TOPK * PEER_TOPK), PEER_TOPK)
        eidx = jnp.take_along_axis(cidx.reshape(B, BLOCK, PEER_HEADS, PEER_TOPK * PEER_TOPK), fpos, -1)
        g = jax.nn.softmax(fv, -1)
        u = expert_u[eidx]
        act = jax.nn.gelu(jnp.einsum('btd,bthkd->bthk', xb, u).astype(jnp.float32), approximate=False)
        w = (g * act).astype(xb.dtype)
        return jnp.einsum('bthk,bthkd->btd', w, expert_v[eidx])

    o = lax.map(one_block, hb)
    return o.transpose(1, 0, 2, 3).reshape(B, S, D)


def setup_inputs(seed: int = 0) -> dict:
    key = jax.random.key(seed)
    ks = jax.random.split(key, 24)
    D = D_MODEL
    nrm = lambda k, shape, s: jax.random.normal(k, shape, jnp.float32) * s
    x = nrm(ks[0], (BATCH, SEQ, D), 1.0)
    c = nrm(ks[1], (BATCH, D), 1.0)
    w_ada = nrm(ks[2], (DEPTH, 2, D, 3 * D), 0.1 * D ** -0.5)
    b_ada = jnp.concatenate([nrm(ks[3], (DEPTH, 2, 2 * D), 0.02), 1.0 + nrm(ks[4], (DEPTH, 2, D), 0.02)], -1)
    ln_g = 1.0 + nrm(ks[5], (DEPTH, 2, D), 0.02)
    ln_b = nrm(ks[6], (DEPTH, 2, D), 0.02)
    even_scale = jnp.asarray(np.concatenate([np.ones(2 * A_WIDTH), np.full(A_WIDTH, DEEPNORM_BETA),
                                             np.ones(2 * B_QK_WIDTH), np.full(B_V_WIDTH, DEEPNORM_BETA)]), jnp.float32)
    w_in_even = nrm(ks[7], (N_EVEN, D, EVEN_IN), D ** -0.5) * even_scale
    rpb = nrm(ks[8], (N_EVEN, NA_HEADS, 2 * NA_WIN_ROWS - 1, 2 * NA_WIN_COLS - 1), 0.02)
    lam_q1 = nrm(ks[9], (N_EVEN, HEAD_DIM), 0.1)
    lam_k1 = nrm(ks[10], (N_EVEN, HEAD_DIM), 0.1)
    lam_q2 = nrm(ks[11], (N_EVEN, HEAD_DIM), 0.1)
    lam_k2 = nrm(ks[12], (N_EVEN, HEAD_DIM), 0.1)
    diff_sub_g = 1.0 + nrm(ks[13], (N_EVEN, 2 * HEAD_DIM), 0.02)
    w_out_even = nrm(ks[14], (N_EVEN, D, D), DEEPNORM_BETA * D ** -0.5)
    odd_scale = jnp.asarray(np.concatenate([np.ones(C_Q + C_KV), np.full(C_KV, DEEPNORM_BETA)]), jnp.float32)
    w_in_odd = nrm(ks[15], (N_ODD, D, ODD_IN), D ** -0.5) * odd_scale
    sink = nrm(ks[16], (N_ODD, SWA_Q_HEADS), 0.5)
    w_out_odd = nrm(ks[17], (N_ODD, D, D), DEEPNORM_BETA * D ** -0.5)
    peer_w_query = nrm(ks[18], (DEPTH, D, PEER_HEADS * PEER_KEY_DIM), D ** -0.5)
    peer_sub_keys = nrm(ks[19], (DEPTH, PEER_HEADS, 2, PEER_N_KEYS, PEER_KEY_DIM // 2), (PEER_KEY_DIM // 2) ** -0.5)
    peer_u = nrm(ks[20], (DEPTH, PEER_N_EXPERTS, D), D ** -0.5)
    peer_v = nrm(ks[21], (DEPTH, PEER_N_EXPERTS, D), DEEPNORM_BETA * PEER_HEADS ** -0.5)
    return {"x": x, "c": c, "w_ada": w_ada, "b_ada": b_ada, "ln_g": ln_g, "ln_b": ln_b,
            "w_in_even": w_in_even, "rpb": rpb, "lam_q1": lam_q1, "lam_k1": lam_k1,
            "lam_q2": lam_q2, "lam_k2": lam_k2, "diff_sub_g": diff_sub_g, "w_out_even": w_out_even,
            "w_in_odd": w_in_odd, "sink": sink, "w_out_odd": w_out_odd,
            "peer_w_query": peer_w_query, "peer_sub_keys": peer_sub_keys, "peer_u": peer_u, "peer_v": peer_v}


def reference(x, c, w_ada, b_ada, ln_g, ln_b, w_in_even, rpb, lam_q1, lam_k1, lam_q2, lam_k2, diff_sub_g,
              w_out_even, w_in_odd, sink, w_out_odd, peer_w_query, peer_sub_keys, peer_u, peer_v):
    B, S, D = x.shape
    cos, sin = rope_tables(S)
    cond = jax.nn.silu(c.astype(jnp.float32)).astype(x.dtype)
    mods = jnp.einsum('bd,lsde->lsbe', cond, w_ada) + b_ada[:, :, None, :]
    even_splits = [A_WIDTH, 2 * A_WIDTH, 3 * A_WIDTH, 3 * A_WIDTH + B_QK_WIDTH, 3 * A_WIDTH + 2 * B_QK_WIDTH]
    odd_splits = [C_Q, C_Q + C_KV]
    for layer in range(DEPTH):
        shift, scale, gate = jnp.split(mods[layer, 0], 3, -1)
        h = x * (1.0 + scale[:, None]) + shift[:, None]
        if layer % 2 == 0:
            i = layer // 2
            proj = jnp.einsum('bsd,de->bse', h, w_in_even[i])
            qa, ka, va, qb, kb, vb = jnp.split(proj, even_splits, -1)
            hd4 = lambda t: t.reshape(B, S, NA_HEADS, HEAD_DIM)
            o_a = neighbourhood_attention(hd4(qa), hd4(ka), hd4(va), rpb[i])
            qb = apply_rope(qb.reshape(B, S, DIFF_HEADS, 2, HEAD_DIM), cos, sin)
            kb = apply_rope(kb.reshape(B, S, DIFF_HEADS, 2, HEAD_DIM), cos, sin)
            lambda_init = 0.8 - 0.6 * math.exp(-0.3 * layer)
            lam = (jnp.exp(jnp.sum(lam_q1[i].astype(jnp.float32) * lam_k1[i].astype(jnp.float32)))
                   - jnp.exp(jnp.sum(lam_q2[i].astype(jnp.float32) * lam_k2[i].astype(jnp.float32))) + lambda_init)
            o_b = diff_attention(qb, kb, vb.reshape(B, S, DIFF_HEADS, 2 * HEAD_DIM), lam, diff_sub_g[i], lambda_init)
            y = jnp.einsum('bse,ed->bsd', jnp.concatenate([o_a, o_b], -1), w_out_even[i])
        else:
            j = layer // 2
            proj = jnp.einsum('bsd,de->bse', h, w_in_odd[j])
            qc, kc, vc = jnp.split(proj, odd_splits, -1)
            qc = apply_rope(qc.reshape(B, S, SWA_Q_HEADS, HEAD_DIM), cos, sin)
            kc = apply_rope(kc.reshape(B, S, SWA_KV_HEADS, HEAD_DIM), cos, sin)
            o_c = sliding_window_gqa(qc, kc, vc.reshape(B, S, SWA_KV_HEADS, HEAD_DIM), sink[j])
            y = jnp.einsum('bse,ed->bsd', o_c, w_out_odd[j])
        x = layer_norm(DEEPNORM_ALPHA * x + gate[:, None] * y, ln_g[layer, 0], ln_b[layer, 0])
        shift2, scale2, gate2 = jnp.split(mods[layer, 1], 3, -1)
        h2 = x * (1.0 + scale2[:, None]) + shift2[:, None]
        y2 = peer(h2, peer_w_query[layer], peer_sub_keys[layer], peer_u[layer], peer_v[layer])
        x = layer_norm(DEEPNORM_ALPHA * x + gate2[:, None] * y2, ln_g[layer, 1], ln_b[layer, 1])
    return x
```

```python
import functools
import math

import numpy as np
import jax
import jax.numpy as jnp
from jax import lax
from jax.experimental import pallas as pl
from jax.experimental.pallas import tpu as pltpu

F32 = jnp.float32
BF16 = jnp.bfloat16

DEPTH = 4
GRID_W = 64
HEAD_DIM = 64
SWA_BLOCK = 128
NA_HEADS = 8
NA_WIN_ROWS = 8
NA_WIN_COLS = 16
DIFF_HEADS = 4
SWA_Q_HEADS = 16
SWA_KV_HEADS = 4
SWA_WINDOW = 128
PEER_HEADS = 8
PEER_N_KEYS = 128
PEER_TOPK = 16
ROPE_THETA = 10000.0
LN_EPS = 1e-5
NEG = -1e30
DEEPNORM_ALPHA = (2 * DEPTH) ** 0.25

A_WIDTH = NA_HEADS * HEAD_DIM
B_WIDTH = DIFF_HEADS * 2 * HEAD_DIM
EVEN_IN = 3 * A_WIDTH + 3 * B_WIDTH
C_Q = SWA_Q_HEADS * HEAD_DIM
C_KV = SWA_KV_HEADS * HEAD_DIM

LANES = 128
PAIR = 2 * HEAD_DIM
MXU_N = 256
VMEM_LIMIT = 48 * 1024 * 1024

NA_QROWS = 4
NA_KROWS = 12


def _params(sem, vmem=VMEM_LIMIT):
    return pltpu.CompilerParams(dimension_semantics=sem, vmem_limit_bytes=vmem)


def _dot(a, b):
    return jnp.dot(a, b, preferred_element_type=F32)


def _dot_nt(a, b):
    return lax.dot_general(a, b, (((1,), (1,)), ((), ())), preferred_element_type=F32)


def _lo_lanes(shape):
    return (lax.broadcasted_iota(jnp.int32, shape, len(shape) - 1) % PAIR) < HEAD_DIM


def _ada_kernel(c_ref, w_ref, b_ref, o_ref):
    cond = jax.nn.silu(c_ref[...])
    o_ref[0] = jnp.dot(cond, w_ref[0], preferred_element_type=F32,
                       precision=lax.Precision.HIGHEST) + b_ref[0]


def _ada_mods(c, w_ada, b_ada):
    B, D = c.shape
    n = w_ada.shape[0] * w_ada.shape[1]
    E = w_ada.shape[-1]
    rows = 8
    cp = jnp.zeros((rows, D), F32).at[:B].set(c.astype(F32))
    tn = 1024
    out = pl.pallas_call(
        _ada_kernel,
        grid=(n, E // tn),
        in_specs=[pl.BlockSpec((rows, D), lambda i, j: (0, 0)),
                  pl.BlockSpec((1, D, tn), lambda i, j: (i, 0, j)),
                  pl.BlockSpec((1, 1, tn), lambda i, j: (i, 0, j))],
        out_specs=pl.BlockSpec((1, rows, tn), lambda i, j: (i, 0, j)),
        out_shape=jax.ShapeDtypeStruct((n, rows, E), F32),
        compiler_params=_params(("parallel", "parallel")),
        name="ada_mods",
    )(cp, w_ada.reshape(n, D, E), b_ada.reshape(n, 1, E))
    return out[:, :B].reshape(w_ada.shape[0], w_ada.shape[1], B, E)


def _mod_proj_kernel(x_ref, shift_ref, scale_ref, w_ref, cs_ref, sn_ref, o_ref, *h_out, sections):
    h = (x_ref[...] * (1.0 + scale_ref[0]) + shift_ref[0]).astype(BF16)
    if h_out:
        h_out[0][...] = h
    tm = h.shape[0]
    reps = MXU_N // LANES
    for (c0, c1, rope, mult) in sections:
        for cc in range(c0, c1, MXU_N):
            acc = _dot(h, w_ref[:, cc:cc + MXU_N])
            if rope:
                cs = jnp.concatenate([cs_ref[...]] * reps, axis=1)
                sn = jnp.concatenate([sn_ref[...]] * reps, axis=1)
                first_half = (lax.broadcasted_iota(jnp.int32, (tm, MXU_N), 1) % HEAD_DIM) < HEAD_DIM // 2
                partner = jnp.where(first_half,
                                    pltpu.roll(acc, MXU_N - HEAD_DIM // 2, 1),
                                    pltpu.roll(acc, HEAD_DIM // 2, 1))
                acc = acc * cs + partner * sn
            if mult != 1.0:
                acc = acc * mult
            o_ref[:, cc:cc + MXU_N] = acc.astype(o_ref.dtype)


def _mod_proj(x, shift, scale, w, cs_tab, sn_tab, sections, S, emit_h=False, tm=512):
    T, D = x.shape
    N = w.shape[1]
    tiles_per_seq = S // tm
    out_shape = [jax.ShapeDtypeStruct((T, N), BF16)]
    out_specs = [pl.BlockSpec((tm, N), lambda i: (i, 0))]
    if emit_h:
        out_shape.append(jax.ShapeDtypeStruct((T, D), BF16))
        out_specs.append(pl.BlockSpec((tm, D), lambda i: (i, 0)))
    res = pl.pallas_call(
        functools.partial(_mod_proj_kernel, sections=sections),
        grid=(T // tm,),
        in_specs=[pl.BlockSpec((tm, D), lambda i: (i, 0)),
                  pl.BlockSpec((1, 1, D), lambda i: (i // tiles_per_seq, 0, 0)),
                  pl.BlockSpec((1, 1, D), lambda i: (i // tiles_per_seq, 0, 0)),
                  pl.BlockSpec((D, N), lambda i: (0, 0)),
                  pl.BlockSpec((tm, LANES), lambda i: (i % tiles_per_seq, 0)),
                  pl.BlockSpec((tm, LANES), lambda i: (i % tiles_per_seq, 0))],
        out_specs=out_specs,
        out_shape=out_shape,
        compiler_params=_params(("parallel",)),
        name="mod_proj",
    )(x, shift, scale, w, cs_tab, sn_tab)
    return res if emit_h else res[0]


def _gated_ln(x, y, gate, g, b):
    z = DEEPNORM_ALPHA * x + gate * y
    mu = jnp.mean(z, axis=-1, keepdims=True)
    zc = z - mu
    var = jnp.mean(zc * zc, axis=-1, keepdims=True)
    return zc * lax.rsqrt(var + LN_EPS) * g + b


def _proj_ln_kernel(*refs, widths):
    n = len(widths)
    o_refs = refs[:n]
    w_ref, x_ref, gate_ref, g_ref, b_ref, out_ref = refs[n:]
    y = None
    k0 = 0
    for o_ref, wd in zip(o_refs, widths):
        part = _dot(o_ref[...], w_ref[k0:k0 + wd, :])
        y = part if y is None else y + part
        k0 += wd
    out_ref[...] = _gated_ln(x_ref[...], y, gate_ref[0], g_ref[...], b_ref[...])


def _proj_ln(o_list, w, x, gate, g, b, S, tm=512):
    T, D = x.shape
    widths = tuple(o.shape[1] for o in o_list)
    tiles_per_seq = S // tm
    in_specs = [pl.BlockSpec((tm, wd), lambda i: (i, 0)) for wd in widths]
    in_specs += [pl.BlockSpec((sum(widths), D), lambda i: (0, 0)),
                 pl.BlockSpec((tm, D), lambda i: (i, 0)),
                 pl.BlockSpec((1, 1, D), lambda i: (i // tiles_per_seq, 0, 0)),
                 pl.BlockSpec((1, D), lambda i: (0, 0)),
                 pl.BlockSpec((1, D), lambda i: (0, 0))]
    return pl.pallas_call(
        functools.partial(_proj_ln_kernel, widths=widths),
        grid=(T // tm,),
        in_specs=in_specs,
        out_specs=pl.BlockSpec((tm, D), lambda i: (i, 0)),
        out_shape=jax.ShapeDtypeStruct((T, D), F32),
        compiler_params=_params(("parallel",)),
        name="proj_ln",
    )(*o_list, w, x, gate, g.reshape(1, D), b.reshape(1, D))


def _na_kernel(q_ref, k_ref, v_ref, bias_ref, o_ref, *, grid_rows):
    blk = pl.program_id(2)
    start_row = jnp.clip(NA_QROWS * blk - NA_WIN_ROWS // 2, 0, grid_rows - NA_KROWS)
    start = pl.multiple_of(start_row * GRID_W, GRID_W)
    nk = NA_KROWS * GRID_W
    kw = k_ref[pl.ds(start, nk), :]
    vw = v_ref[pl.ds(start, nk), :]
    q = q_ref[...]
    lo_q = _lo_lanes(q.shape)
    lo_v = _lo_lanes(vw.shape)
    out = None
    for c in range(2):
        qm = jnp.where(lo_q if c == 0 else ~lo_q, q, jnp.zeros_like(q))
        s = _dot_nt(qm, kw) + bias_ref[0, c]
        m = jnp.max(s, axis=1, keepdims=True)
        e = jnp.exp(s - m)
        p = (e / jnp.sum(e, axis=1, keepdims=True)).astype(BF16)
        vm = jnp.where(lo_v if c == 0 else ~lo_v, vw, jnp.zeros_like(vw))
        part = _dot(p, vm)
        out = part if out is None else out + part
    o_ref[...] = out.astype(o_ref.dtype)


def _na_bias_tables(rpb_i, grid_rows):
    nblk = grid_rows // NA_QROWS
    nq, nk = NA_QROWS * GRID_W, NA_KROWS * GRID_W
    tabs = []
    for blk in (0, 1, nblk - 1):
        start_row = int(np.clip(NA_QROWS * blk - NA_WIN_ROWS // 2, 0, grid_rows - NA_KROWS))
        qi, ki = np.arange(nq), np.arange(nk)
        qr, qc = NA_QROWS * blk + qi // GRID_W, qi % GRID_W
        kr, kc = start_row + ki // GRID_W, ki % GRID_W
        row_start = np.clip(qr - NA_WIN_ROWS // 2, 0, grid_rows - NA_WIN_ROWS)
        col_start = np.clip(qc - NA_WIN_COLS // 2, 0, GRID_W - NA_WIN_COLS)
        ok = ((kr[None, :] >= row_start[:, None]) & (kr[None, :] < row_start[:, None] + NA_WIN_ROWS)
              & (kc[None, :] >= col_start[:, None]) & (kc[None, :] < col_start[:, None] + NA_WIN_COLS))
        dr = np.clip(kr[None, :] - qr[:, None] + NA_WIN_ROWS - 1, 0, 2 * NA_WIN_ROWS - 2)
        dc = np.clip(kc[None, :] - qc[:, None] + NA_WIN_COLS - 1, 0, 2 * NA_WIN_COLS - 2)
        bias = rpb_i.astype(F32)[:, dr, dc]
        tabs.append(jnp.where(jnp.asarray(ok)[None], bias, NEG))
    return jnp.stack(tabs)


def _na_attention(proj, rpb_i, B, S):
    T = B * S
    grid_rows = S // GRID_W
    nblk = grid_rows // NA_QROWS
    nq, nk = NA_QROWS * GRID_W, NA_KROWS * GRID_W
    npairs = A_WIDTH // PAIR
    bias = _na_bias_tables(rpb_i, grid_rows)

    def cls(j):
        return jnp.where(j == 0, 0, jnp.where(j == nblk - 1, 2, 1))

    return pl.pallas_call(
        functools.partial(_na_kernel, grid_rows=grid_rows),
        grid=(B, npairs, nblk),
        in_specs=[pl.BlockSpec((nq, PAIR), lambda b, p, j: (b * nblk + j, p)),
                  pl.BlockSpec((S, PAIR), lambda b, p, j: (b, npairs + p)),
                  pl.BlockSpec((S, PAIR), lambda b, p, j: (b, 2 * npairs + p)),
                  pl.BlockSpec((1, 2, nq, nk), lambda b, p, j: (cls(j), p, 0, 0))],
        out_specs=pl.BlockSpec((nq, PAIR), lambda b, p, j: (b * nblk + j, p)),
        out_shape=jax.ShapeDtypeStruct((T, A_WIDTH), BF16),
        compiler_params=_params(("parallel", "parallel", "arbitrary")),
        name="na_attention",
    )(proj, proj, proj, bias)


def _diff_kernel(q_ref, k_ref, v_ref, lamp_ref, subg_ref, o_ref, m_ref, l_ref, acc_ref, *, lambda_init):
    kv = pl.program_id(3)

    @pl.when(kv == 0)
    def _():
        m_ref[...] = jnp.full(m_ref.shape, -jnp.inf, F32)
        l_ref[...] = jnp.zeros(l_ref.shape, F32)
        acc_ref[...] = jnp.zeros(acc_ref.shape, F32)

    q = q_ref[...]
    k = k_ref[...]
    v = v_ref[...]
    lo = _lo_lanes(q.shape)
    reps = k.shape[0] // LANES
    for c in range(2):
        qm = jnp.where(lo if c == 0 else ~lo, q, jnp.zeros_like(q))
        s = _dot_nt(qm, k)
        m_prev = m_ref[c]
        m_new = jnp.maximum(m_prev, jnp.max(s, axis=1, keepdims=True))
        alpha = jnp.exp(m_prev - m_new)
        p = jnp.exp(s - jnp.concatenate([m_new] * reps, axis=1))
        l_ref[c] = alpha * l_ref[c] + jnp.sum(p, axis=1, keepdims=True)
        acc_ref[c] = alpha * acc_ref[c] + _dot(p.astype(BF16), v)
        m_ref[c] = m_new

    @pl.when(kv == pl.num_programs(3) - 1)
    def _():
        lp = lamp_ref[...]
        lam = (jnp.exp(jnp.sum(lp[0:1] * lp[1:2], axis=1, keepdims=True))
               - jnp.exp(jnp.sum(lp[2:3] * lp[3:4], axis=1, keepdims=True)) + lambda_init)
        o = acc_ref[0] / l_ref[0] - lam * (acc_ref[1] / l_ref[1])
        o = o * lax.rsqrt(jnp.mean(o * o, axis=-1, keepdims=True) + LN_EPS) * subg_ref[...]
        o_ref[...] = (o * (1.0 - lambda_init)).astype(o_ref.dtype)


def _diff_attention(proj, lam_params, sub_g, lambda_init, B, S, tq=512, tk=512):
    T = B * S
    off = 3 * A_WIDTH // PAIR
    nh = DIFF_HEADS
    nq, nkv = S // tq, S // tk
    return pl.pallas_call(
        functools.partial(_diff_kernel, lambda_init=lambda_init),
        grid=(B, nh, nq, nkv),
        in_specs=[pl.BlockSpec((tq, PAIR), lambda b, h, i, j: (b * nq + i, off + h)),
                  pl.BlockSpec((tk, PAIR), lambda b, h, i, j: (b * nkv + j, off + nh + h)),
                  pl.BlockSpec((tk, PAIR), lambda b, h, i, j: (b * nkv + j, off + 2 * nh + h)),
                  pl.BlockSpec((4, HEAD_DIM), lambda b, h, i, j: (0, 0)),
                  pl.BlockSpec((1, PAIR), lambda b, h, i, j: (0, 0))],
        out_specs=pl.BlockSpec((tq, PAIR), lambda b, h, i, j: (b * nq + i, h)),
        out_shape=jax.ShapeDtypeStruct((T, B_WIDTH), BF16),
        scratch_shapes=[pltpu.VMEM((2, tq, LANES), F32),
                        pltpu.VMEM((2, tq, LANES), F32),
                        pltpu.VMEM((2, tq, PAIR), F32)],
        compiler_params=_params(("parallel", "parallel", "parallel", "arbitrary")),
        name="diff_attention",
    )(proj, proj, proj, lam_params, sub_g.reshape(1, PAIR))


def _swa_kernel(sink_ref, q_ref, kp_ref, km_ref, kn_ref, vp_ref, vm_ref, vn_ref, o_ref, kbuf, vbuf, *, nblocks):
    kvh = pl.program_id(1)
    qt = pl.program_id(2)
    blk = SWA_BLOCK
    nsub = q_ref.shape[0] // blk
    group = SWA_Q_HEADS // SWA_KV_HEADS
    kbuf[0:blk] = kp_ref[...]
    kbuf[blk:blk + nsub * blk] = km_ref[...]
    kbuf[blk + nsub * blk:] = kn_ref[...]
    vbuf[0:blk] = vp_ref[...]
    vbuf[blk:blk + nsub * blk] = vm_ref[...]
    vbuf[blk + nsub * blk:] = vn_ref[...]

    qi = lax.broadcasted_iota(jnp.int32, (blk, 3 * blk), 0)
    kk = lax.broadcasted_iota(jnp.int32, (blk, 3 * blk), 1)
    band = (kk - qi >= blk - SWA_WINDOW) & (kk - qi <= blk + SWA_WINDOW)
    lo = _lo_lanes((blk, PAIR))
    for j in range(nsub):
        n = qt * nsub + j
        first_key = jnp.where(n > 0, 0, blk)
        end_key = jnp.where(n < nblocks - 1, 3 * blk, 2 * blk)
        ok = band & (kk >= first_key) & (kk < end_key)
        kw = kbuf[j * blk:(j + 3) * blk]
        vw = vbuf[j * blk:(j + 3) * blk]
        qs = []
        for pr in range(group // 2):
            qp = q_ref[j * blk:(j + 1) * blk, pr * PAIR:(pr + 1) * PAIR]
            qs.append(jnp.where(lo, qp, jnp.zeros_like(qp)))
            qs.append(jnp.where(lo, jnp.zeros_like(qp), qp))
        s_all = _dot_nt(jnp.concatenate(qs, axis=0), kw)
        ps = []
        for g in range(group):
            sink = sink_ref[kvh * group + g]
            s = jnp.where(ok, s_all[g * blk:(g + 1) * blk], NEG)
            m = jnp.maximum(jnp.max(s, axis=1, keepdims=True), sink)
            e = jnp.exp(s - m)
            den = jnp.sum(e, axis=1, keepdims=True) + jnp.exp(sink - m)
            ps.append((e / den).astype(BF16))
        o_all = _dot(jnp.concatenate(ps, axis=0), vw)
        for pr in range(group // 2):
            o_pair = jnp.where(lo, o_all[(2 * pr) * blk:(2 * pr + 1) * blk],
                               o_all[(2 * pr + 1) * blk:(2 * pr + 2) * blk])
            o_ref[j * blk:(j + 1) * blk, pr * PAIR:(pr + 1) * PAIR] = o_pair.astype(o_ref.dtype)


def _swa_attention(proj, sink_j, B, S, tq=512):
    T = B * S
    blk = SWA_BLOCK
    nsub = tq // blk
    nqt = S // tq
    nblocks = S // blk
    group = SWA_Q_HEADS // SWA_KV_HEADS
    qw = group * HEAD_DIM
    koff = C_Q // PAIR
    voff = koff + SWA_KV_HEADS

    def prev_map(off):
        return lambda b, h, i: (b * nblocks + jnp.maximum(i * nsub - 1, 0), off + h)

    def main_map(off):
        return lambda b, h, i: (b * nqt + i, off + h)

    def next_map(off):
        return lambda b, h, i: (b * nblocks + jnp.minimum(i * nsub + nsub, nblocks - 1), off + h)

    return pl.pallas_call(
        functools.partial(_swa_kernel, nblocks=nblocks),
        grid=(B, SWA_KV_HEADS, nqt),
        in_specs=[pl.BlockSpec(memory_space=pltpu.SMEM),
                  pl.BlockSpec((tq, qw), lambda b, h, i: (b * nqt + i, h)),
                  pl.BlockSpec((blk, PAIR), prev_map(koff)),
                  pl.BlockSpec((tq, PAIR), main_map(koff)),
                  pl.BlockSpec((blk, PAIR), next_map(koff)),
                  pl.BlockSpec((blk, PAIR), prev_map(voff)),
                  pl.BlockSpec((tq, PAIR), main_map(voff)),
                  pl.BlockSpec((blk, PAIR), next_map(voff))],
        out_specs=pl.BlockSpec((tq, qw), lambda b, h, i: (b * nqt + i, h)),
        out_shape=jax.ShapeDtypeStruct((T, C_Q), BF16),
        scratch_shapes=[pltpu.VMEM((tq + 2 * blk, PAIR), BF16),
                        pltpu.VMEM((tq + 2 * blk, PAIR), BF16)],
        compiler_params=_params(("parallel", "parallel", "parallel")),
        name="swa_attention",
    )(sink_j.astype(F32), proj, proj, proj, proj, proj, proj, proj)


def _extract_topk(s, k):
    nrows = s.shape[0]
    row = lax.broadcasted_iota(jnp.int32, s.shape, 0)
    rank = jnp.full(s.shape, k, jnp.int32)
    vals = []
    for r in range(k):
        m = jnp.max(s, axis=0, keepdims=True)
        first = jnp.min(jnp.where(s == m, row, nrows), axis=0, keepdims=True)
        hit = row == first
        rank = jnp.where(hit, r, rank)
        s = jnp.where(hit, -jnp.inf, s)
        vals.append(m)
    return rank, jnp.concatenate(vals, axis=0)


def _peer_route_kernel(q_ref, keys_ref, n_ref, ea_ref, rb_ref, eb_ref):
    K = PEER_TOPK
    half = q_ref.shape[1] // 2
    q = q_ref[...]
    sa = _dot_nt(keys_ref[0, 0], q[:, :half])
    sb = _dot_nt(keys_ref[0, 1], q[:, half:])
    rank_a, va = _extract_topk(sa, K)
    rank_b, vb = _extract_topk(sb, K)

    ninf = jnp.full((8, sa.shape[1]), -jnp.inf, F32)
    sub = lax.broadcasted_iota(jnp.int32, ninf.shape, 0)
    pieces = [va[0:1] + vb]
    for ki in range(1, 8):
        pieces.append(jnp.where(sub < K // (ki + 1), va[ki:ki + 1] + vb[0:8], ninf))
    pieces.append(va[8:16] + vb[0:1])
    cand = jnp.concatenate(pieces, axis=0)
    rank_c, vc = _extract_topk(cand, K)
    sel = (rank_c < K).astype(F32)
    counts = [jnp.sum(sel[0:16], axis=0, keepdims=True)]
    for ki in range(1, 8):
        counts.append(jnp.sum(sel[8 + 8 * ki:16 + 8 * ki], axis=0, keepdims=True))
    counts.append(sel[72:80])
    nvec = jnp.concatenate(counts, axis=0)
    z = jnp.sum(jnp.exp(vc - vc[0:1]), axis=0, keepdims=True)

    n_of_i = jnp.zeros(sa.shape, F32)
    for ki in range(K):
        n_of_i = jnp.where(rank_a == ki, nvec[ki:ki + 1], n_of_i)
    n_ref[0] = n_of_i
    ea_ref[0] = jnp.where(rank_a < K, jnp.exp(sa - va[0:1]) / z, 0.0)
    rb_ref[0] = rank_b.astype(F32)
    eb_ref[0] = jnp.where(rank_b < K, jnp.exp(sb - vb[0:1]), 0.0)


def _peer_route(q, sub_keys, tt=256):
    T = q.shape[0]
    H = PEER_HEADS
    kd = q.shape[1] // H
    shp = jax.ShapeDtypeStruct((H, PEER_N_KEYS, T), F32)
    spec = pl.BlockSpec((1, PEER_N_KEYS, tt), lambda i, h: (h, 0, i))
    return pl.pallas_call(
        _peer_route_kernel,
        grid=(T // tt, H),
        in_specs=[pl.BlockSpec((tt, kd), lambda i, h: (i, h)),
                  pl.BlockSpec((1, 2, PEER_N_KEYS, kd // 2), lambda i, h: (h, 0, 0, 0))],
        out_specs=[spec, spec, spec, spec],
        out_shape=[shp, shp, shp, shp],
        compiler_params=_params(("parallel", "parallel")),
        name="peer_route",
    )(q, sub_keys)


def _peer_dense_kernel(h_ref, u_ref, vt_ref, n_ref, ea_ref, rb_ref, eb_ref, x_ref, gate_ref, g_ref, b_ref,
                       out_ref, acc_ref, w_ref):
    et = pl.program_id(1)

    @pl.when(et == 0)
    def _():
        acc_ref[...] = jnp.zeros(acc_ref.shape, F32)

    nk = PEER_N_KEYS
    ni = u_ref.shape[0] // nk
    pre = _dot_nt(u_ref[...], h_ref[...])
    act = 0.5 * pre * (1.0 + lax.erf(pre * np.float32(math.sqrt(0.5))))
    for ii in range(ni):
        i = et * ni + ii
        gate = None
        for h in range(PEER_HEADS):
            n_i = n_ref[h, pl.ds(i, 1), :]
            ea_i = ea_ref[h, pl.ds(i, 1), :]
            part = jnp.where(rb_ref[h] < n_i, eb_ref[h] * ea_i, 0.0)
            gate = part if gate is None else gate + part
        w_ref[ii * nk:(ii + 1) * nk, :] = (gate * act[ii * nk:(ii + 1) * nk]).astype(BF16)
    acc_ref[...] += _dot(vt_ref[...], w_ref[...])

    @pl.when(et == pl.num_programs(1) - 1)
    def _():
        y = acc_ref[...].T
        out_ref[...] = _gated_ln(x_ref[...], y, gate_ref[0], g_ref[...], b_ref[...])


def _peer_dense(h, u, vt, routing, x, gate, g, b, S, tt=512, te=512):
    T, D = x.shape
    E = u.shape[0]
    H, nk = PEER_HEADS, PEER_N_KEYS
    tiles_per_seq = S // tt
    rspec = pl.BlockSpec((H, nk, tt), lambda i, e: (0, 0, i))
    return pl.pallas_call(
        _peer_dense_kernel,
        grid=(T // tt, E // te),
        in_specs=[pl.BlockSpec((tt, D), lambda i, e: (i, 0)),
                  pl.BlockSpec((te, D), lambda i, e: (e, 0)),
                  pl.BlockSpec((D, te), lambda i, e: (0, e)),
                  rspec, rspec, rspec, rspec,
                  pl.BlockSpec((tt, D), lambda i, e: (i, 0)),
                  pl.BlockSpec((1, 1, D), lambda i, e: (i // tiles_per_seq, 0, 0)),
                  pl.BlockSpec((1, D), lambda i, e: (0, 0)),
                  pl.BlockSpec((1, D), lambda i, e: (0, 0))],
        out_specs=pl.BlockSpec((tt, D), lambda i, e: (i, 0)),
        out_shape=jax.ShapeDtypeStruct((T, D), F32),
        scratch_shapes=[pltpu.VMEM((D, tt), F32), pltpu.VMEM((te, tt), BF16)],
        compiler_params=_params(("parallel", "arbitrary")),
        name="peer_dense",
    )(h, u, vt, *routing, x, gate, g.reshape(1, D), b.reshape(1, D))


def _rope_tables(S):
    inv = 1.0 / (ROPE_THETA ** (jnp.arange(0, HEAD_DIM, 2, dtype=F32) / HEAD_DIM))
    ang = jnp.arange(S, dtype=F32)[:, None] * inv[None, :]
    cos, sin = jnp.cos(ang), jnp.sin(ang)
    reps = LANES // HEAD_DIM
    cs = jnp.concatenate([cos, cos] * reps, axis=1)
    sn = jnp.concatenate([-sin, sin] * reps, axis=1)
    return cs, sn


def _split3(m):
    B = m.shape[0]
    D = m.shape[1] // 3
    return (m[:, :D].reshape(B, 1, D), m[:, D:2 * D].reshape(B, 1, D), m[:, 2 * D:].reshape(B, 1, D))


def kernel(x, c, w_ada, b_ada, ln_g, ln_b, w_in_even, rpb, lam_q1, lam_k1, lam_q2, lam_k2, diff_sub_g,
           w_out_even, w_in_odd, sink, w_out_odd, peer_w_query, peer_sub_keys, peer_u, peer_v):
    B, S, D = x.shape
    T = B * S
    qscale = HEAD_DIM ** -0.5
    cs_tab, sn_tab = _rope_tables(S)
    mods = _ada_mods(c, w_ada, b_ada)
    xf = x.reshape(T, D).astype(F32)

    even_sections = ((0, A_WIDTH, False, qscale),
                     (A_WIDTH, 3 * A_WIDTH, False, 1.0),
                     (3 * A_WIDTH, 3 * A_WIDTH + B_WIDTH, True, qscale),
                     (3 * A_WIDTH + B_WIDTH, 3 * A_WIDTH + 2 * B_WIDTH, True, 1.0),
                     (3 * A_WIDTH + 2 * B_WIDTH, EVEN_IN, False, 1.0))
    dup = np.repeat(np.arange(SWA_KV_HEADS), 2)[:, None] * HEAD_DIM + np.arange(HEAD_DIM)[None, :]
    odd_cols = np.concatenate([np.arange(C_Q), C_Q + dup.reshape(-1), C_Q + C_KV + dup.reshape(-1)])
    odd_sections = ((0, C_Q, True, qscale),
                    (C_Q, C_Q + 2 * C_KV, True, 1.0),
                    (C_Q + 2 * C_KV, C_Q + 4 * C_KV, False, 1.0))
    peer_sections = ((0, peer_w_query.shape[-1], False, 1.0),)

    for layer in range(DEPTH):
        shift, scale, gate = _split3(mods[layer, 0])
        if layer % 2 == 0:
            i = layer // 2
            proj = _mod_proj(xf, shift, scale, w_in_even[i].astype(BF16), cs_tab, sn_tab, even_sections, S)
            o_a = _na_attention(proj, rpb[i], B, S)
            lambda_init = 0.8 - 0.6 * math.exp(-0.3 * layer)
            lam_params = jnp.stack([lam_q1[i], lam_k1[i], lam_q2[i], lam_k2[i]]).astype(F32)
            o_b = _diff_attention(proj, lam_params, diff_sub_g[i].astype(F32), lambda_init, B, S)
            xf = _proj_ln([o_a, o_b], w_out_even[i].astype(BF16), xf, gate, ln_g[layer, 0], ln_b[layer, 0], S)
        else:
            j = layer // 2
            proj = _mod_proj(xf, shift, scale, w_in_odd[j][:, odd_cols].astype(BF16), cs_tab, sn_tab,
                             odd_sections, S)
            o_c = _swa_attention(proj, sink[j], B, S)
            xf = _proj_ln([o_c], w_out_odd[j].astype(BF16), xf, gate, ln_g[layer, 0], ln_b[layer, 0], S)
        shift2, scale2, gate2 = _split3(mods[layer, 1])
        q, h2 = _mod_proj(xf, shift2, scale2, peer_w_query[layer].astype(BF16), cs_tab, sn_tab,
                          peer_sections, S, emit_h=True)
        routing = _peer_route(q, peer_sub_keys[layer].astype(BF16))
        xf = _peer_dense(h2, peer_u[layer].astype(BF16), peer_v[layer].T.astype(BF16), routing,
                         xf, gate2, ln_g[layer, 1], ln_b[layer, 1], S)
    return xf.reshape(B, S, D).astype(x.dtype)
```

```python
import functools
import math

import numpy as np
import jax
import jax.numpy as jnp
from jax import lax
from jax.experimental import pallas as pl
from jax.experimental.pallas import tpu as pltpu

F32 = jnp.float32
BF16 = jnp.bfloat16

DEPTH = 4
GRID_W = 64
HEAD_DIM = 64
SWA_BLOCK = 128
NA_HEADS = 8
NA_WIN_ROWS = 8
NA_WIN_COLS = 16
DIFF_HEADS = 4
SWA_Q_HEADS = 16
SWA_KV_HEADS = 4
SWA_WINDOW = 128
PEER_HEADS = 8
PEER_N_KEYS = 128
PEER_TOPK = 16
ROPE_THETA = 10000.0
LN_EPS = 1e-5
NEG = -1e30
DEEPNORM_ALPHA = (2 * DEPTH) ** 0.25

A_WIDTH = NA_HEADS * HEAD_DIM
B_WIDTH = DIFF_HEADS * 2 * HEAD_DIM
EVEN_IN = 3 * A_WIDTH + 3 * B_WIDTH
C_Q = SWA_Q_HEADS * HEAD_DIM
C_KV = SWA_KV_HEADS * HEAD_DIM

LANES = 128
PAIR = 2 * HEAD_DIM
MXU_N = 256
VMEM_LIMIT = 48 * 1024 * 1024

NA_QROWS = 4
NA_KROWS = 12


def _params(sem, vmem=VMEM_LIMIT):
    return pltpu.CompilerParams(dimension_semantics=sem, vmem_limit_bytes=vmem)


def _dot(a, b):
    return jnp.dot(a, b, preferred_element_type=F32)


def _dot_nt(a, b):
    return lax.dot_general(a, b, (((1,), (1,)), ((), ())), preferred_element_type=F32)


def _lo_lanes(shape):
    return (lax.broadcasted_iota(jnp.int32, shape, len(shape) - 1) % PAIR) < HEAD_DIM


def _ada_kernel(c_ref, w_ref, b_ref, o_ref):
    cond = jax.nn.silu(c_ref[...])
    o_ref[0] = jnp.dot(cond, w_ref[0], preferred_element_type=F32,
                       precision=lax.Precision.HIGHEST) + b_ref[0]


def _ada_mods(c, w_ada, b_ada):
    B, D = c.shape
    n = w_ada.shape[0] * w_ada.shape[1]
    E = w_ada.shape[-1]
    rows = 8
    cp = jnp.zeros((rows, D), F32).at[:B].set(c.astype(F32))
    tn = 1024
    out = pl.pallas_call(
        _ada_kernel,
        grid=(n, E // tn),
        in_specs=[pl.BlockSpec((rows, D), lambda i, j: (0, 0)),
                  pl.BlockSpec((1, D, tn), lambda i, j: (i, 0, j)),
                  pl.BlockSpec((1, 1, tn), lambda i, j: (i, 0, j))],
        out_specs=pl.BlockSpec((1, rows, tn), lambda i, j: (i, 0, j)),
        out_shape=jax.ShapeDtypeStruct((n, rows, E), F32),
        compiler_params=_params(("parallel", "parallel")),
        name="ada_mods",
    )(cp, w_ada.reshape(n, D, E), b_ada.reshape(n, 1, E))
    return out[:, :B].reshape(w_ada.shape[0], w_ada.shape[1], B, E)


def _mod_proj_kernel(x_ref, shift_ref, scale_ref, w_ref, cs_ref, sn_ref, o_ref, *h_out, sections):
    h = (x_ref[...] * (1.0 + scale_ref[0]) + shift_ref[0]).astype(BF16)
    if h_out:
        h_out[0][...] = h
    tm = h.shape[0]
    reps = MXU_N // LANES
    for (c0, c1, rope, mult) in sections:
        for cc in range(c0, c1, MXU_N):
            acc = _dot(h, w_ref[:, cc:cc + MXU_N])
            if rope:
                cs = jnp.concatenate([cs_ref[...]] * reps, axis=1)
                sn = jnp.concatenate([sn_ref[...]] * reps, axis=1)
                first_half = (lax.broadcasted_iota(jnp.int32, (tm, MXU_N), 1) % HEAD_DIM) < HEAD_DIM // 2
                partner = jnp.where(first_half,
                                    pltpu.roll(acc, MXU_N - HEAD_DIM // 2, 1),
                                    pltpu.roll(acc, HEAD_DIM // 2, 1))
                acc = acc * cs + partner * sn
            if mult != 1.0:
                acc = acc * mult
            o_ref[:, cc:cc + MXU_N] = acc.astype(o_ref.dtype)


def _mod_proj(x, shift, scale, w, cs_tab, sn_tab, sections, S, emit_h=False, tm=512):
    T, D = x.shape
    N = w.shape[1]
    tiles_per_seq = S // tm
    out_shape = [jax.ShapeDtypeStruct((T, N), BF16)]
    out_specs = [pl.BlockSpec((tm, N), lambda i: (i, 0))]
    if emit_h:
        out_shape.append(jax.ShapeDtypeStruct((T, D), BF16))
        out_specs.append(pl.BlockSpec((tm, D), lambda i: (i, 0)))
    res = pl.pallas_call(
        functools.partial(_mod_proj_kernel, sections=sections),
        grid=(T // tm,),
        in_specs=[pl.BlockSpec((tm, D), lambda i: (i, 0)),
                  pl.BlockSpec((1, 1, D), lambda i: (i // tiles_per_seq, 0, 0)),
                  pl.BlockSpec((1, 1, D), lambda i: (i // tiles_per_seq, 0, 0)),
                  pl.BlockSpec((D, N), lambda i: (0, 0)),
                  pl.BlockSpec((tm, LANES), lambda i: (i % tiles_per_seq, 0)),
                  pl.BlockSpec((tm, LANES), lambda i: (i % tiles_per_seq, 0))],
        out_specs=out_specs,
        out_shape=out_shape,
        compiler_params=_params(("parallel",)),
        name="mod_proj",
    )(x, shift, scale, w, cs_tab, sn_tab)
    return res if emit_h else res[0]


def _gated_ln(x, y, gate, g, b):
    z = DEEPNORM_ALPHA * x + gate * y
    mu = jnp.mean(z, axis=-1, keepdims=True)
    zc = z - mu
    var = jnp.mean(zc * zc, axis=-1, keepdims=True)
    return zc * lax.rsqrt(var + LN_EPS) * g + b


def _proj_ln_kernel(*refs, widths):
    n = len(widths)
    o_refs = refs[:n]
    w_ref, x_ref, gate_ref, g_ref, b_ref, out_ref = refs[n:]
    y = None
    k0 = 0
    for o_ref, wd in zip(o_refs, widths):
        part = _dot(o_ref[...], w_ref[k0:k0 + wd, :])
        y = part if y is None else y + part
        k0 += wd
    out_ref[...] = _gated_ln(x_ref[...], y, gate_ref[0], g_ref[...], b_ref[...])


def _proj_ln(o_list, w, x, gate, g, b, S, tm=512):
    T, D = x.shape
    widths = tuple(o.shape[1] for o in o_list)
    tiles_per_seq = S // tm
    in_specs = [pl.BlockSpec((tm, wd), lambda i: (i, 0)) for wd in widths]
    in_specs += [pl.BlockSpec((sum(widths), D), lambda i: (0, 0)),
                 pl.BlockSpec((tm, D), lambda i: (i, 0)),
                 pl.BlockSpec((1, 1, D), lambda i: (i // tiles_per_seq, 0, 0)),
                 pl.BlockSpec((1, D), lambda i: (0, 0)),
                 pl.BlockSpec((1, D), lambda i: (0, 0))]
    return pl.pallas_call(
        functools.partial(_proj_ln_kernel, widths=widths),
        grid=(T // tm,),
        in_specs=in_specs,
        out_specs=pl.BlockSpec((tm, D), lambda i: (i, 0)),
        out_shape=jax.ShapeDtypeStruct((T, D), F32),
        compiler_params=_params(("parallel",)),
        name="proj_ln",
    )(*o_list, w, x, gate, g.reshape(1, D), b.reshape(1, D))


def _na_kernel(q_ref, k_ref, v_ref, bias_ref, o_ref, *, grid_rows):
    blk = pl.program_id(2)
    start_row = jnp.clip(NA_QROWS * blk - NA_WIN_ROWS // 2, 0, grid_rows - NA_KROWS)
    start = pl.multiple_of(start_row * GRID_W, GRID_W)
    nk = NA_KROWS * GRID_W
    kw = k_ref[pl.ds(start, nk), :]
    vw = v_ref[pl.ds(start, nk), :]
    q = q_ref[...]
    lo_q = _lo_lanes(q.shape)
    lo_v = _lo_lanes(vw.shape)
    out = None
    for c in range(2):
        qm = jnp.where(lo_q if c == 0 else ~lo_q, q, jnp.zeros_like(q))
        s = _dot_nt(qm, kw) + bias_ref[0, c]
        m = jnp.max(s, axis=1, keepdims=True)
        e = jnp.exp(s - m)
        p = (e / jnp.sum(e, axis=1, keepdims=True)).astype(BF16)
        vm = jnp.where(lo_v if c == 0 else ~lo_v, vw, jnp.zeros_like(vw))
        part = _dot(p, vm)
        out = part if out is None else out + part
    o_ref[...] = out.astype(o_ref.dtype)


def _na_bias_tables(rpb_i, grid_rows):
    nblk = grid_rows // NA_QROWS
    ndr, ndc = 2 * NA_WIN_ROWS - 1, 2 * NA_WIN_COLS - 1
    col = np.arange(GRID_W)
    col_start = np.clip(col - NA_WIN_COLS // 2, 0, GRID_W - NA_WIN_COLS)
    col_ok = (col[None, :] >= col_start[:, None]) & (col[None, :] < col_start[:, None] + NA_WIN_COLS)
    dc = np.clip(col[None, :] - col[:, None] + NA_WIN_COLS - 1, 0, ndc - 1)
    pick = (dc[None] == np.arange(ndc)[:, None, None]).astype(np.float32)
    blocks = jnp.einsum('hrd,dqk->hrqk', rpb_i.astype(F32), jnp.asarray(pick), precision=lax.Precision.HIGHEST)
    blocks = jnp.where(jnp.asarray(col_ok)[None, None], blocks, NEG)
    blocks = jnp.concatenate([blocks, jnp.full((NA_HEADS, 1, GRID_W, GRID_W), NEG, F32)], axis=1)
    tabs = []
    for blk in (0, 1, nblk - 1):
        start_row = int(np.clip(NA_QROWS * blk - NA_WIN_ROWS // 2, 0, grid_rows - NA_KROWS))
        qr = NA_QROWS * blk + np.arange(NA_QROWS)
        kr = start_row + np.arange(NA_KROWS)
        row_start = np.clip(qr - NA_WIN_ROWS // 2, 0, grid_rows - NA_WIN_ROWS)
        row_ok = (kr[None, :] >= row_start[:, None]) & (kr[None, :] < row_start[:, None] + NA_WIN_ROWS)
        dr = np.where(row_ok, kr[None, :] - qr[:, None] + NA_WIN_ROWS - 1, ndr)
        tab = jnp.concatenate(
            [jnp.concatenate([blocks[:, int(dr[a, b])] for b in range(NA_KROWS)], axis=-1)
             for a in range(NA_QROWS)], axis=-2)
        tabs.append(tab)
    return jnp.stack(tabs)


def _na_attention(proj, rpb_i, B, S):
    T = B * S
    grid_rows = S // GRID_W
    nblk = grid_rows // NA_QROWS
    nq, nk = NA_QROWS * GRID_W, NA_KROWS * GRID_W
    npairs = A_WIDTH // PAIR
    bias = _na_bias_tables(rpb_i, grid_rows)

    def cls(j):
        return jnp.where(j == 0, 0, jnp.where(j == nblk - 1, 2, 1))

    return pl.pallas_call(
        functools.partial(_na_kernel, grid_rows=grid_rows),
        grid=(B, npairs, nblk),
        in_specs=[pl.BlockSpec((nq, PAIR), lambda b, p, j: (b * nblk + j, p)),
                  pl.BlockSpec((S, PAIR), lambda b, p, j: (b, npairs + p)),
                  pl.BlockSpec((S, PAIR), lambda b, p, j: (b, 2 * npairs + p)),
                  pl.BlockSpec((1, 2, nq, nk), lambda b, p, j: (cls(j), p, 0, 0))],
        out_specs=pl.BlockSpec((nq, PAIR), lambda b, p, j: (b * nblk + j, p)),
        out_shape=jax.ShapeDtypeStruct((T, A_WIDTH), BF16),
        compiler_params=_params(("parallel", "parallel", "arbitrary")),
        name="na_attention",
    )(proj, proj, proj, bias)


def _diff_kernel(q_ref, k_ref, v_ref, lamp_ref, subg_ref, o_ref, m_ref, l_ref, acc_ref, *, lambda_init):
    kv = pl.program_id(3)

    @pl.when(kv == 0)
    def _():
        m_ref[...] = jnp.full(m_ref.shape, -jnp.inf, F32)
        l_ref[...] = jnp.zeros(l_ref.shape, F32)
        acc_ref[...] = jnp.zeros(acc_ref.shape, F32)

    q = q_ref[...]
    k = k_ref[...]
    v = v_ref[...]
    lo = _lo_lanes(q.shape)
    reps = k.shape[0] // LANES
    for c in range(2):
        qm = jnp.where(lo if c == 0 else ~lo, q, jnp.zeros_like(q))
        s = _dot_nt(qm, k)
        m_prev = m_ref[c]
        m_new = jnp.maximum(m_prev, jnp.max(s, axis=1, keepdims=True))
        alpha = jnp.exp2(m_prev - m_new)
        p = jnp.exp2(s - jnp.concatenate([m_new] * reps, axis=1))
        l_ref[c] = alpha * l_ref[c] + jnp.sum(p, axis=1, keepdims=True)
        acc_ref[c] = alpha * acc_ref[c] + _dot(p.astype(BF16), v)
        m_ref[c] = m_new

    @pl.when(kv == pl.num_programs(3) - 1)
    def _():
        lp = lamp_ref[...]
        lam = (jnp.exp(jnp.sum(lp[0:1] * lp[1:2], axis=1, keepdims=True))
               - jnp.exp(jnp.sum(lp[2:3] * lp[3:4], axis=1, keepdims=True)) + lambda_init)
        o = acc_ref[0] / l_ref[0] - lam * (acc_ref[1] / l_ref[1])
        o = o * lax.rsqrt(jnp.mean(o * o, axis=-1, keepdims=True) + LN_EPS) * subg_ref[...]
        o_ref[...] = (o * (1.0 - lambda_init)).astype(o_ref.dtype)


def _diff_attention(proj, lam_params, sub_g, lambda_init, B, S, tq=1024, tk=1024):
    T = B * S
    off = 3 * A_WIDTH // PAIR
    nh = DIFF_HEADS
    nq, nkv = S // tq, S // tk
    return pl.pallas_call(
        functools.partial(_diff_kernel, lambda_init=lambda_init),
        grid=(B, nh, nq, nkv),
        in_specs=[pl.BlockSpec((tq, PAIR), lambda b, h, i, j: (b * nq + i, off + h)),
                  pl.BlockSpec((tk, PAIR), lambda b, h, i, j: (b * nkv + j, off + nh + h)),
                  pl.BlockSpec((tk, PAIR), lambda b, h, i, j: (b * nkv + j, off + 2 * nh + h)),
                  pl.BlockSpec((4, HEAD_DIM), lambda b, h, i, j: (0, 0)),
                  pl.BlockSpec((1, PAIR), lambda b, h, i, j: (0, 0))],
        out_specs=pl.BlockSpec((tq, PAIR), lambda b, h, i, j: (b * nq + i, h)),
        out_shape=jax.ShapeDtypeStruct((T, B_WIDTH), BF16),
        scratch_shapes=[pltpu.VMEM((2, tq, LANES), F32),
                        pltpu.VMEM((2, tq, LANES), F32),
                        pltpu.VMEM((2, tq, PAIR), F32)],
        compiler_params=_params(("parallel", "parallel", "parallel", "arbitrary")),
        name="diff_attention",
    )(proj, proj, proj, lam_params, sub_g.reshape(1, PAIR))


def _swa_kernel(sink_ref, q_ref, kp_ref, km_ref, kn_ref, vp_ref, vm_ref, vn_ref, o_ref, kbuf, vbuf, *, nblocks):
    kvh = pl.program_id(1)
    qt = pl.program_id(2)
    blk = SWA_BLOCK
    nsub = q_ref.shape[0] // blk
    group = SWA_Q_HEADS // SWA_KV_HEADS
    kbuf[0:blk] = kp_ref[...]
    kbuf[blk:blk + nsub * blk] = km_ref[...]
    kbuf[blk + nsub * blk:] = kn_ref[...]
    vbuf[0:blk] = vp_ref[...]
    vbuf[blk:blk + nsub * blk] = vm_ref[...]
    vbuf[blk + nsub * blk:] = vn_ref[...]

    qi = lax.broadcasted_iota(jnp.int32, (blk, 3 * blk), 0)
    kk = lax.broadcasted_iota(jnp.int32, (blk, 3 * blk), 1)
    band = (kk - qi >= blk - SWA_WINDOW) & (kk - qi <= blk + SWA_WINDOW)
    lo = _lo_lanes((blk, PAIR))
    for j in range(nsub):
        n = qt * nsub + j
        first_key = jnp.where(n > 0, 0, blk)
        end_key = jnp.where(n < nblocks - 1, 3 * blk, 2 * blk)
        ok = band & (kk >= first_key) & (kk < end_key)
        kw = kbuf[j * blk:(j + 3) * blk]
        vw = vbuf[j * blk:(j + 3) * blk]
        qs = []
        for pr in range(group // 2):
            qp = q_ref[j * blk:(j + 1) * blk, pr * PAIR:(pr + 1) * PAIR]
            qs.append(jnp.where(lo, qp, jnp.zeros_like(qp)))
            qs.append(jnp.where(lo, jnp.zeros_like(qp), qp))
        s_all = _dot_nt(jnp.concatenate(qs, axis=0), kw)
        ps = []
        for g in range(group):
            sink = sink_ref[kvh * group + g]
            s = jnp.where(ok, s_all[g * blk:(g + 1) * blk], NEG)
            m = jnp.maximum(jnp.max(s, axis=1, keepdims=True), sink)
            e = jnp.exp(s - m)
            den = jnp.sum(e, axis=1, keepdims=True) + jnp.exp(sink - m)
            ps.append((e / den).astype(BF16))
        o_all = _dot(jnp.concatenate(ps, axis=0), vw)
        for pr in range(group // 2):
            o_pair = jnp.where(lo, o_all[(2 * pr) * blk:(2 * pr + 1) * blk],
                               o_all[(2 * pr + 1) * blk:(2 * pr + 2) * blk])
            o_ref[j * blk:(j + 1) * blk, pr * PAIR:(pr + 1) * PAIR] = o_pair.astype(o_ref.dtype)


def _swa_attention(proj, sink_j, B, S, tq=512):
    T = B * S
    blk = SWA_BLOCK
    nsub = tq // blk
    nqt = S // tq
    nblocks = S // blk
    group = SWA_Q_HEADS // SWA_KV_HEADS
    qw = group * HEAD_DIM
    koff = C_Q // PAIR
    voff = koff + SWA_KV_HEADS

    def prev_map(off):
        return lambda b, h, i: (b * nblocks + jnp.maximum(i * nsub - 1, 0), off + h)

    def main_map(off):
        return lambda b, h, i: (b * nqt + i, off + h)

    def next_map(off):
        return lambda b, h, i: (b * nblocks + jnp.minimum(i * nsub + nsub, nblocks - 1), off + h)

    return pl.pallas_call(
        functools.partial(_swa_kernel, nblocks=nblocks),
        grid=(B, SWA_KV_HEADS, nqt),
        in_specs=[pl.BlockSpec(memory_space=pltpu.SMEM),
                  pl.BlockSpec((tq, qw), lambda b, h, i: (b * nqt + i, h)),
                  pl.BlockSpec((blk, PAIR), prev_map(koff)),
                  pl.BlockSpec((tq, PAIR), main_map(koff)),
                  pl.BlockSpec((blk, PAIR), next_map(koff)),
                  pl.BlockSpec((blk, PAIR), prev_map(voff)),
                  pl.BlockSpec((tq, PAIR), main_map(voff)),
                  pl.BlockSpec((blk, PAIR), next_map(voff))],
        out_specs=pl.BlockSpec((tq, qw), lambda b, h, i: (b * nqt + i, h)),
        out_shape=jax.ShapeDtypeStruct((T, C_Q), BF16),
        scratch_shapes=[pltpu.VMEM((tq + 2 * blk, PAIR), BF16),
                        pltpu.VMEM((tq + 2 * blk, PAIR), BF16)],
        compiler_params=_params(("parallel", "parallel", "parallel")),
        name="swa_attention",
    )(sink_j.astype(F32), proj, proj, proj, proj, proj, proj, proj)


def _extract_topk(s, k):
    nrows = s.shape[0]
    row = lax.broadcasted_iota(jnp.int32, s.shape, 0).astype(F32)
    rank = jnp.full(s.shape, float(k), F32)
    vals = []
    for r in range(k):
        m = jnp.max(s, axis=0, keepdims=True)
        first = jnp.min(jnp.where(s == m, row, float(nrows)), axis=0, keepdims=True)
        hit = row == first
        rank = jnp.where(hit, float(r), rank)
        s = jnp.where(hit, -jnp.inf, s)
        vals.append(m)
    return rank, jnp.concatenate(vals, axis=0)


def _peer_route_kernel(q_ref, keys_ref, n_ref, ea_ref, rb_ref, eb_ref):
    K = PEER_TOPK
    half = q_ref.shape[1] // 2
    q = q_ref[...]
    sa = _dot_nt(keys_ref[0, 0], q[:, :half])
    sb = _dot_nt(keys_ref[0, 1], q[:, half:])
    rank_a, va = _extract_topk(sa, K)
    rank_b, vb = _extract_topk(sb, K)

    ninf = jnp.full((8, sa.shape[1]), -jnp.inf, F32)
    sub = lax.broadcasted_iota(jnp.int32, ninf.shape, 0)
    pieces = [va[0:1] + vb]
    for ki in range(1, 8):
        pieces.append(jnp.where(sub < K // (ki + 1), va[ki:ki + 1] + vb[0:8], ninf))
    pieces.append(va[8:16] + vb[0:1])
    cand = jnp.concatenate(pieces, axis=0)
    rank_c, vc = _extract_topk(cand, K)
    sel = (rank_c < K).astype(F32)
    counts = [jnp.sum(sel[0:16], axis=0, keepdims=True)]
    for ki in range(1, 8):
        counts.append(jnp.sum(sel[8 + 8 * ki:16 + 8 * ki], axis=0, keepdims=True))
    counts.append(sel[72:80])
    nvec = jnp.concatenate(counts, axis=0)
    z = jnp.sum(jnp.exp(vc - vc[0:1]), axis=0, keepdims=True)

    n_of_i = jnp.zeros(sa.shape, F32)
    for ki in range(K):
        n_of_i = jnp.where(rank_a == ki, nvec[ki:ki + 1], n_of_i)
    n_ref[0] = n_of_i
    ea_ref[0] = jnp.where(rank_a < K, jnp.exp(sa - va[0:1]) / z, 0.0)
    rb_ref[0] = pltpu.bitcast(rank_b.astype(BF16), jnp.int32)
    eb_ref[0] = pltpu.bitcast(jnp.where(rank_b < K, jnp.exp(sb - vb[0:1]), 0.0).astype(BF16), jnp.int32)


def _peer_route(q, sub_keys, tt=256):
    T = q.shape[0]
    H = PEER_HEADS
    kd = q.shape[1] // H
    shp = jax.ShapeDtypeStruct((H, PEER_N_KEYS, T), F32)
    shp16 = jax.ShapeDtypeStruct((H, PEER_N_KEYS // 2, T), jnp.int32)
    spec = pl.BlockSpec((1, PEER_N_KEYS, tt), lambda i, h: (h, 0, i))
    spec16 = pl.BlockSpec((1, PEER_N_KEYS // 2, tt), lambda i, h: (h, 0, i))
    return pl.pallas_call(
        _peer_route_kernel,
        grid=(T // tt, H),
        in_specs=[pl.BlockSpec((tt, kd), lambda i, h: (i, h)),
                  pl.BlockSpec((1, 2, PEER_N_KEYS, kd // 2), lambda i, h: (h, 0, 0, 0))],
        out_specs=[spec, spec, spec16, spec16],
        out_shape=[shp, shp, shp16, shp16],
        compiler_params=_params(("parallel", "parallel")),
        name="peer_route",
    )(q, sub_keys)


def _peer_dense_kernel(h_ref, u_ref, vt_ref, n_ref, ea_ref, rb_ref, eb_ref, x_ref, gate_ref, g_ref, b_ref,
                       out_ref, acc_ref, pre_ref, w_ref):
    et = pl.program_id(1)

    @pl.when(et == 0)
    def _():
        acc_ref[...] = jnp.zeros(acc_ref.shape, F32)

    nk = PEER_N_KEYS
    ni = u_ref.shape[0] // nk
    nlc = h_ref.shape[0] // LANES
    pack = 16
    assert ni == 8, "the per-i rows of one expert tile are read as one aligned 8-sublane block"
    i0 = pl.multiple_of(et * ni, ni)
    half = nk // 2

    def bcast_rows(blk, ii):
        row = jnp.broadcast_to(blk[ii:ii + 1], (pack, LANES)).astype(BF16)
        return jnp.concatenate([row] * (half // pack), axis=0)

    pre_ref[...] = _dot_nt(u_ref[...], h_ref[...])

    def gate_block(it, carry):
        lanes = pl.ds(pl.multiple_of((it // 2) * LANES, LANES), LANES)
        jh = it % 2
        jwords = pl.ds(pl.multiple_of(jh * (half // 2), half // 2), half // 2)
        gates = [None] * ni
        for h in range(PEER_HEADS):
            n_blk = n_ref[h, pl.ds(i0, ni), lanes]
            ea_blk = ea_ref[h, pl.ds(i0, ni), lanes]
            rb = pltpu.bitcast(rb_ref[h, jwords, lanes], BF16)
            eb = pltpu.bitcast(eb_ref[h, jwords, lanes], BF16)
            for ii in range(ni):
                part = jnp.where(rb < bcast_rows(n_blk, ii), eb * bcast_rows(ea_blk, ii), 0)
                gates[ii] = part if gates[ii] is None else gates[ii] + part
        for ii in range(ni):
            rows = pl.ds(pl.multiple_of(ii * nk + jh * half, half), half)
            pre = pre_ref[rows, lanes]
            act = 0.5 * pre * (1.0 + lax.erf(pre * np.float32(math.sqrt(0.5))))
            w_ref[rows, lanes] = gates[ii] * act.astype(BF16)
        return carry

    lax.fori_loop(0, 2 * nlc, gate_block, 0)
    acc_ref[...] += _dot(vt_ref[...], w_ref[...])

    @pl.when(et == pl.num_programs(1) - 1)
    def _():
        y = acc_ref[...].T
        out_ref[...] = _gated_ln(x_ref[...], y, gate_ref[0], g_ref[...], b_ref[...])


def _peer_dense(h, u, vt, routing, x, gate, g, b, S, tt=512, te=1024):
    T, D = x.shape
    E = u.shape[0]
    H, nk = PEER_HEADS, PEER_N_KEYS
    tiles_per_seq = S // tt
    rspec = pl.BlockSpec((H, nk, tt), lambda i, e: (0, 0, i))
    rspec16 = pl.BlockSpec((H, nk // 2, tt), lambda i, e: (0, 0, i))
    return pl.pallas_call(
        _peer_dense_kernel,
        grid=(T // tt, E // te),
        in_specs=[pl.BlockSpec((tt, D), lambda i, e: (i, 0)),
                  pl.BlockSpec((te, D), lambda i, e: (e, 0)),
                  pl.BlockSpec((D, te), lambda i, e: (0, e)),
                  rspec, rspec, rspec16, rspec16,
                  pl.BlockSpec((tt, D), lambda i, e: (i, 0)),
                  pl.BlockSpec((1, 1, D), lambda i, e: (i // tiles_per_seq, 0, 0)),
                  pl.BlockSpec((1, D), lambda i, e: (0, 0)),
                  pl.BlockSpec((1, D), lambda i, e: (0, 0))],
        out_specs=pl.BlockSpec((tt, D), lambda i, e: (i, 0)),
        out_shape=jax.ShapeDtypeStruct((T, D), F32),
        scratch_shapes=[pltpu.VMEM((D, tt), F32), pltpu.VMEM((te, tt), F32), pltpu.VMEM((te, tt), BF16)],
        compiler_params=_params(("parallel", "arbitrary")),
        name="peer_dense",
    )(h, u, vt, *routing, x, gate, g.reshape(1, D), b.reshape(1, D))


def _rope_tables(S):
    inv = 1.0 / (ROPE_THETA ** (jnp.arange(0, HEAD_DIM, 2, dtype=F32) / HEAD_DIM))
    ang = jnp.arange(S, dtype=F32)[:, None] * inv[None, :]
    cos, sin = jnp.cos(ang), jnp.sin(ang)
    reps = LANES // HEAD_DIM
    cs = jnp.concatenate([cos, cos] * reps, axis=1)
    sn = jnp.concatenate([-sin, sin] * reps, axis=1)
    return cs, sn


def _split3(m):
    B = m.shape[0]
    D = m.shape[1] // 3
    return (m[:, :D].reshape(B, 1, D), m[:, D:2 * D].reshape(B, 1, D), m[:, 2 * D:].reshape(B, 1, D))


def kernel(x, c, w_ada, b_ada, ln_g, ln_b, w_in_even, rpb, lam_q1, lam_k1, lam_q2, lam_k2, diff_sub_g,
           w_out_even, w_in_odd, sink, w_out_odd, peer_w_query, peer_sub_keys, peer_u, peer_v):
    B, S, D = x.shape
    T = B * S
    qscale = HEAD_DIM ** -0.5
    cs_tab, sn_tab = _rope_tables(S)
    mods = _ada_mods(c, w_ada, b_ada)
    xf = x.reshape(T, D).astype(F32)

    even_sections = ((0, A_WIDTH, False, qscale),
                     (A_WIDTH, 3 * A_WIDTH, False, 1.0),
                     (3 * A_WIDTH, 3 * A_WIDTH + B_WIDTH, True, qscale * math.log2(math.e)),
                     (3 * A_WIDTH + B_WIDTH, 3 * A_WIDTH + 2 * B_WIDTH, True, 1.0),
                     (3 * A_WIDTH + 2 * B_WIDTH, EVEN_IN, False, 1.0))
    dup = np.repeat(np.arange(SWA_KV_HEADS), 2)[:, None] * HEAD_DIM + np.arange(HEAD_DIM)[None, :]
    odd_cols = np.concatenate([np.arange(C_Q), C_Q + dup.reshape(-1), C_Q + C_KV + dup.reshape(-1)])
    odd_sections = ((0, C_Q, True, qscale),
                    (C_Q, C_Q + 2 * C_KV, True, 1.0),
                    (C_Q + 2 * C_KV, C_Q + 4 * C_KV, False, 1.0))
    peer_sections = ((0, peer_w_query.shape[-1], False, 1.0),)

    for layer in range(DEPTH):
        shift, scale, gate = _split3(mods[layer, 0])
        if layer % 2 == 0:
            i = layer // 2
            proj = _mod_proj(xf, shift, scale, w_in_even[i].astype(BF16), cs_tab, sn_tab, even_sections, S)
            o_a = _na_attention(proj, rpb[i], B, S)
            lambda_init = 0.8 - 0.6 * math.exp(-0.3 * layer)
            lam_params = jnp.stack([lam_q1[i], lam_k1[i], lam_q2[i], lam_k2[i]]).astype(F32)
            o_b = _diff_attention(proj, lam_params, diff_sub_g[i].astype(F32), lambda_init, B, S)
            xf = _proj_ln([o_a, o_b], w_out_even[i].astype(BF16), xf, gate, ln_g[layer, 0], ln_b[layer, 0], S)
        else:
            j = layer // 2
            proj = _mod_proj(xf, shift, scale, w_in_odd[j][:, odd_cols].astype(BF16), cs_tab, sn_tab,
                             odd_sections, S)
            o_c = _swa_attention(proj, sink[j], B, S)
            xf = _proj_ln([o_c], w_out_odd[j].astype(BF16), xf, gate, ln_g[layer, 0], ln_b[layer, 0], S)
        shift2, scale2, gate2 = _split3(mods[layer, 1])
        q, h2 = _mod_proj(xf, shift2, scale2, peer_w_query[layer].astype(BF16), cs_tab, sn_tab,
                          peer_sections, S, emit_h=True)
        routing = _peer_route(q, peer_sub_keys[layer].astype(BF16))
        xf = _peer_dense(h2, peer_u[layer].astype(BF16), peer_v[layer].T.astype(BF16), routing,
                         xf, gate2, ln_g[layer, 1], ln_b[layer, 1], S)
    return xf.reshape(B, S, D).astype(x.dtype)
```

```python
import functools
import math

import numpy as np
import jax
import jax.numpy as jnp
from jax import lax
from jax.experimental import pallas as pl
from jax.experimental.pallas import tpu as pltpu

F32 = jnp.float32
BF16 = jnp.bfloat16

DEPTH = 4
GRID_W = 64
HEAD_DIM = 64
SWA_BLOCK = 128
NA_HEADS = 8
NA_WIN_ROWS = 8
NA_WIN_COLS = 16
DIFF_HEADS = 4
SWA_Q_HEADS = 16
SWA_KV_HEADS = 4
SWA_WINDOW = 128
PEER_HEADS = 8
PEER_N_KEYS = 128
PEER_TOPK = 16
ROPE_THETA = 10000.0
LN_EPS = 1e-5
NEG = -1e30
DEEPNORM_ALPHA = (2 * DEPTH) ** 0.25
GELU_IN_SCALE = math.sqrt(0.5)

A_WIDTH = NA_HEADS * HEAD_DIM
B_WIDTH = DIFF_HEADS * 2 * HEAD_DIM
EVEN_IN = 3 * A_WIDTH + 3 * B_WIDTH
C_Q = SWA_Q_HEADS * HEAD_DIM
C_KV = SWA_KV_HEADS * HEAD_DIM

LANES = 128
PAIR = 2 * HEAD_DIM
MXU_N = 256
VMEM_LIMIT = 48 * 1024 * 1024

NA_QROWS = 4
NA_KROWS = 12


def _params(sem, vmem=VMEM_LIMIT):
    return pltpu.CompilerParams(dimension_semantics=sem, vmem_limit_bytes=vmem)


def _dot(a, b):
    return jnp.dot(a, b, preferred_element_type=F32)


def _dot_nt(a, b):
    return lax.dot_general(a, b, (((1,), (1,)), ((), ())), preferred_element_type=F32)


def _lo_lanes(shape):
    return (lax.broadcasted_iota(jnp.int32, shape, len(shape) - 1) % PAIR) < HEAD_DIM


def _ada_kernel(c_ref, w_ref, b_ref, o_ref):
    cond = jax.nn.silu(c_ref[...])
    o_ref[0] = jnp.dot(cond, w_ref[0], preferred_element_type=F32,
                       precision=lax.Precision.HIGHEST) + b_ref[0]


def _ada_mods(c, w_ada, b_ada):
    B, D = c.shape
    n = w_ada.shape[0] * w_ada.shape[1]
    E = w_ada.shape[-1]
    rows = 8
    cp = jnp.zeros((rows, D), F32).at[:B].set(c.astype(F32))
    tn = 1024
    out = pl.pallas_call(
        _ada_kernel,
        grid=(n, E // tn),
        in_specs=[pl.BlockSpec((rows, D), lambda i, j: (0, 0)),
                  pl.BlockSpec((1, D, tn), lambda i, j: (i, 0, j)),
                  pl.BlockSpec((1, 1, tn), lambda i, j: (i, 0, j))],
        out_specs=pl.BlockSpec((1, rows, tn), lambda i, j: (i, 0, j)),
        out_shape=jax.ShapeDtypeStruct((n, rows, E), F32),
        compiler_params=_params(("parallel", "parallel")),
        name="ada_mods",
    )(cp, w_ada.reshape(n, D, E), b_ada.reshape(n, 1, E))
    return out[:, :B].reshape(w_ada.shape[0], w_ada.shape[1], B, E)


def _mod_proj_kernel(x_ref, shift_ref, scale_ref, w_ref, cs_ref, sn_ref, o_ref, *h_out, sections, h_mult):
    hf = x_ref[...] * (1.0 + scale_ref[0]) + shift_ref[0]
    h = hf.astype(BF16)
    if h_out:
        h_out[0][...] = (hf * h_mult).astype(BF16)
    tm = h.shape[0]
    reps = MXU_N // LANES
    for (c0, c1, rope, mult) in sections:
        for cc in range(c0, c1, MXU_N):
            acc = _dot(h, w_ref[:, cc:cc + MXU_N])
            if rope:
                cs = jnp.concatenate([cs_ref[...]] * reps, axis=1)
                sn = jnp.concatenate([sn_ref[...]] * reps, axis=1)
                first_half = (lax.broadcasted_iota(jnp.int32, (tm, MXU_N), 1) % HEAD_DIM) < HEAD_DIM // 2
                partner = jnp.where(first_half,
                                    pltpu.roll(acc, MXU_N - HEAD_DIM // 2, 1),
                                    pltpu.roll(acc, HEAD_DIM // 2, 1))
                acc = acc * cs + partner * sn
            if mult != 1.0:
                acc = acc * mult
            o_ref[:, cc:cc + MXU_N] = acc.astype(o_ref.dtype)


def _mod_proj(x, shift, scale, w, cs_tab, sn_tab, sections, S, emit_h=False, h_mult=1.0, tm=512):
    T, D = x.shape
    N = w.shape[1]
    tiles_per_seq = S // tm
    out_shape = [jax.ShapeDtypeStruct((T, N), BF16)]
    out_specs = [pl.BlockSpec((tm, N), lambda i: (i, 0))]
    if emit_h:
        out_shape.append(jax.ShapeDtypeStruct((T, D), BF16))
        out_specs.append(pl.BlockSpec((tm, D), lambda i: (i, 0)))
    res = pl.pallas_call(
        functools.partial(_mod_proj_kernel, sections=sections, h_mult=h_mult),
        grid=(T // tm,),
        in_specs=[pl.BlockSpec((tm, D), lambda i: (i, 0)),
                  pl.BlockSpec((1, 1, D), lambda i: (i // tiles_per_seq, 0, 0)),
                  pl.BlockSpec((1, 1, D), lambda i: (i // tiles_per_seq, 0, 0)),
                  pl.BlockSpec((D, N), lambda i: (0, 0)),
                  pl.BlockSpec((tm, LANES), lambda i: (i % tiles_per_seq, 0)),
                  pl.BlockSpec((tm, LANES), lambda i: (i % tiles_per_seq, 0))],
        out_specs=out_specs,
        out_shape=out_shape,
        compiler_params=_params(("parallel",)),
        name="mod_proj",
    )(x, shift, scale, w, cs_tab, sn_tab)
    return res if emit_h else res[0]


def _gated_ln(x, y, gate, g, b):
    z = DEEPNORM_ALPHA * x + gate * y
    mu = jnp.mean(z, axis=-1, keepdims=True)
    zc = z - mu
    var = jnp.mean(zc * zc, axis=-1, keepdims=True)
    return zc * lax.rsqrt(var + LN_EPS) * g + b


def _proj_ln_kernel(*refs, widths):
    n = len(widths)
    o_refs = refs[:n]
    w_ref, x_ref, gate_ref, g_ref, b_ref, out_ref = refs[n:]
    y = None
    k0 = 0
    for o_ref, wd in zip(o_refs, widths):
        part = _dot(o_ref[...], w_ref[k0:k0 + wd, :])
        y = part if y is None else y + part
        k0 += wd
    out_ref[...] = _gated_ln(x_ref[...], y, gate_ref[0], g_ref[...], b_ref[...])


def _proj_ln(o_list, w, x, gate, g, b, S, tm=512):
    T, D = x.shape
    widths = tuple(o.shape[1] for o in o_list)
    tiles_per_seq = S // tm
    in_specs = [pl.BlockSpec((tm, wd), lambda i: (i, 0)) for wd in widths]
    in_specs += [pl.BlockSpec((sum(widths), D), lambda i: (0, 0)),
                 pl.BlockSpec((tm, D), lambda i: (i, 0)),
                 pl.BlockSpec((1, 1, D), lambda i: (i // tiles_per_seq, 0, 0)),
                 pl.BlockSpec((1, D), lambda i: (0, 0)),
                 pl.BlockSpec((1, D), lambda i: (0, 0))]
    return pl.pallas_call(
        functools.partial(_proj_ln_kernel, widths=widths),
        grid=(T // tm,),
        in_specs=in_specs,
        out_specs=pl.BlockSpec((tm, D), lambda i: (i, 0)),
        out_shape=jax.ShapeDtypeStruct((T, D), F32),
        compiler_params=_params(("parallel",)),
        name="proj_ln",
    )(*o_list, w, x, gate, g.reshape(1, D), b.reshape(1, D))


def _na_kernel(q_ref, k_ref, v_ref, bias_ref, o_ref, *, grid_rows):
    blk = pl.program_id(2)
    start_row = jnp.clip(NA_QROWS * blk - NA_WIN_ROWS // 2, 0, grid_rows - NA_KROWS)
    start = pl.multiple_of(start_row * GRID_W, GRID_W)
    nk = NA_KROWS * GRID_W
    kw = k_ref[pl.ds(start, nk), :]
    vw = v_ref[pl.ds(start, nk), :]
    q = q_ref[...]
    lo_q = _lo_lanes(q.shape)
    lo_v = _lo_lanes(vw.shape)
    out = None
    for c in range(2):
        qm = jnp.where(lo_q if c == 0 else ~lo_q, q, jnp.zeros_like(q))
        s = _dot_nt(qm, kw) + bias_ref[0, c]
        m = jnp.max(s, axis=1, keepdims=True)
        e = jnp.exp(s - m)
        p = (e / jnp.sum(e, axis=1, keepdims=True)).astype(BF16)
        vm = jnp.where(lo_v if c == 0 else ~lo_v, vw, jnp.zeros_like(vw))
        part = _dot(p, vm)
        out = part if out is None else out + part
    o_ref[...] = out.astype(o_ref.dtype)


def _na_bias_tables(rpb_i, grid_rows):
    nblk = grid_rows // NA_QROWS
    ndr, ndc = 2 * NA_WIN_ROWS - 1, 2 * NA_WIN_COLS - 1
    col = np.arange(GRID_W)
    col_start = np.clip(col - NA_WIN_COLS // 2, 0, GRID_W - NA_WIN_COLS)
    col_ok = (col[None, :] >= col_start[:, None]) & (col[None, :] < col_start[:, None] + NA_WIN_COLS)
    dc = np.clip(col[None, :] - col[:, None] + NA_WIN_COLS - 1, 0, ndc - 1)
    pick = (dc[None] == np.arange(ndc)[:, None, None]).astype(np.float32)
    blocks = jnp.einsum('hrd,dqk->hrqk', rpb_i.astype(F32), jnp.asarray(pick), precision=lax.Precision.HIGHEST)
    blocks = jnp.where(jnp.asarray(col_ok)[None, None], blocks, NEG)
    blocks = jnp.concatenate([blocks, jnp.full((NA_HEADS, 1, GRID_W, GRID_W), NEG, F32)], axis=1)
    tabs = []
    for blk in (0, 1, nblk - 1):
        start_row = int(np.clip(NA_QROWS * blk - NA_WIN_ROWS // 2, 0, grid_rows - NA_KROWS))
        qr = NA_QROWS * blk + np.arange(NA_QROWS)
        kr = start_row + np.arange(NA_KROWS)
        row_start = np.clip(qr - NA_WIN_ROWS // 2, 0, grid_rows - NA_WIN_ROWS)
        row_ok = (kr[None, :] >= row_start[:, None]) & (kr[None, :] < row_start[:, None] + NA_WIN_ROWS)
        dr = np.where(row_ok, kr[None, :] - qr[:, None] + NA_WIN_ROWS - 1, ndr)
        tab = jnp.concatenate(
            [jnp.concatenate([blocks[:, int(dr[a, b])] for b in range(NA_KROWS)], axis=-1)
             for a in range(NA_QROWS)], axis=-2)
        tabs.append(tab)
    return jnp.stack(tabs)


def _na_attention(proj, rpb_i, B, S):
    T = B * S
    grid_rows = S // GRID_W
    nblk = grid_rows // NA_QROWS
    nq, nk = NA_QROWS * GRID_W, NA_KROWS * GRID_W
    npairs = A_WIDTH // PAIR
    bias = _na_bias_tables(rpb_i, grid_rows)

    def cls(j):
        return jnp.where(j == 0, 0, jnp.where(j == nblk - 1, 2, 1))

    return pl.pallas_call(
        functools.partial(_na_kernel, grid_rows=grid_rows),
        grid=(B, npairs, nblk),
        in_specs=[pl.BlockSpec((nq, PAIR), lambda b, p, j: (b * nblk + j, p)),
                  pl.BlockSpec((S, PAIR), lambda b, p, j: (b, npairs + p)),
                  pl.BlockSpec((S, PAIR), lambda b, p, j: (b, 2 * npairs + p)),
                  pl.BlockSpec((1, 2, nq, nk), lambda b, p, j: (cls(j), p, 0, 0))],
        out_specs=pl.BlockSpec((nq, PAIR), lambda b, p, j: (b * nblk + j, p)),
        out_shape=jax.ShapeDtypeStruct((T, A_WIDTH), BF16),
        compiler_params=_params(("parallel", "parallel", "arbitrary")),
        name="na_attention",
    )(proj, proj, proj, bias)


def _diff_kernel(q_ref, k_ref, v_ref, lamp_ref, subg_ref, o_ref, m_ref, l_ref, acc_ref, *, lambda_init):
    kv = pl.program_id(3)

    @pl.when(kv == 0)
    def _():
        m_ref[...] = jnp.full(m_ref.shape, -jnp.inf, F32)
        l_ref[...] = jnp.zeros(l_ref.shape, F32)
        acc_ref[...] = jnp.zeros(acc_ref.shape, F32)

    q = q_ref[...]
    k = k_ref[...]
    v = v_ref[...]
    lo = _lo_lanes(q.shape)
    reps = k.shape[0] // LANES
    for c in range(2):
        qm = jnp.where(lo if c == 0 else ~lo, q, jnp.zeros_like(q))
        s = _dot_nt(qm, k)
        m_prev = m_ref[c]
        m_new = jnp.maximum(m_prev, jnp.max(s, axis=1, keepdims=True))
        alpha = jnp.exp2(m_prev - m_new)
        p = jnp.exp2(s - jnp.concatenate([m_new] * reps, axis=1))
        l_ref[c] = alpha * l_ref[c] + jnp.sum(p, axis=1, keepdims=True)
        acc_ref[c] = alpha * acc_ref[c] + _dot(p.astype(BF16), v)
        m_ref[c] = m_new

    @pl.when(kv == pl.num_programs(3) - 1)
    def _():
        lp = lamp_ref[...]
        lam = (jnp.exp(jnp.sum(lp[0:1] * lp[1:2], axis=1, keepdims=True))
               - jnp.exp(jnp.sum(lp[2:3] * lp[3:4], axis=1, keepdims=True)) + lambda_init)
        o = acc_ref[0] / l_ref[0] - lam * (acc_ref[1] / l_ref[1])
        o = o * lax.rsqrt(jnp.mean(o * o, axis=-1, keepdims=True) + LN_EPS) * subg_ref[...]
        o_ref[...] = (o * (1.0 - lambda_init)).astype(o_ref.dtype)


def _diff_attention(proj, lam_params, sub_g, lambda_init, B, S, tq=1024, tk=1024):
    T = B * S
    off = 3 * A_WIDTH // PAIR
    nh = DIFF_HEADS
    nq, nkv = S // tq, S // tk
    return pl.pallas_call(
        functools.partial(_diff_kernel, lambda_init=lambda_init),
        grid=(B, nh, nq, nkv),
        in_specs=[pl.BlockSpec((tq, PAIR), lambda b, h, i, j: (b * nq + i, off + h)),
                  pl.BlockSpec((tk, PAIR), lambda b, h, i, j: (b * nkv + j, off + nh + h)),
                  pl.BlockSpec((tk, PAIR), lambda b, h, i, j: (b * nkv + j, off + 2 * nh + h)),
                  pl.BlockSpec((4, HEAD_DIM), lambda b, h, i, j: (0, 0)),
                  pl.BlockSpec((1, PAIR), lambda b, h, i, j: (0, 0))],
        out_specs=pl.BlockSpec((tq, PAIR), lambda b, h, i, j: (b * nq + i, h)),
        out_shape=jax.ShapeDtypeStruct((T, B_WIDTH), BF16),
        scratch_shapes=[pltpu.VMEM((2, tq, LANES), F32),
                        pltpu.VMEM((2, tq, LANES), F32),
                        pltpu.VMEM((2, tq, PAIR), F32)],
        compiler_params=_params(("parallel", "parallel", "parallel", "arbitrary")),
        name="diff_attention",
    )(proj, proj, proj, lam_params, sub_g.reshape(1, PAIR))


def _swa_kernel(sink_ref, q_ref, kp_ref, km_ref, kn_ref, vp_ref, vm_ref, vn_ref, o_ref, kbuf, vbuf, *, nblocks):
    kvh = pl.program_id(1)
    qt = pl.program_id(2)
    blk = SWA_BLOCK
    nsub = q_ref.shape[0] // blk
    group = SWA_Q_HEADS // SWA_KV_HEADS
    kbuf[0:blk] = kp_ref[...]
    kbuf[blk:blk + nsub * blk] = km_ref[...]
    kbuf[blk + nsub * blk:] = kn_ref[...]
    vbuf[0:blk] = vp_ref[...]
    vbuf[blk:blk + nsub * blk] = vm_ref[...]
    vbuf[blk + nsub * blk:] = vn_ref[...]

    qi = lax.broadcasted_iota(jnp.int32, (blk, 3 * blk), 0)
    kk = lax.broadcasted_iota(jnp.int32, (blk, 3 * blk), 1)
    band = (kk - qi >= blk - SWA_WINDOW) & (kk - qi <= blk + SWA_WINDOW)
    lo = _lo_lanes((blk, PAIR))
    for j in range(nsub):
        n = qt * nsub + j
        first_key = jnp.where(n > 0, 0, blk)
        end_key = jnp.where(n < nblocks - 1, 3 * blk, 2 * blk)
        ok = band & (kk >= first_key) & (kk < end_key)
        kw = kbuf[j * blk:(j + 3) * blk]
        vw = vbuf[j * blk:(j + 3) * blk]
        qs = []
        for pr in range(group // 2):
            qp = q_ref[j * blk:(j + 1) * blk, pr * PAIR:(pr + 1) * PAIR]
            qs.append(jnp.where(lo, qp, jnp.zeros_like(qp)))
            qs.append(jnp.where(lo, jnp.zeros_like(qp), qp))
        s_all = _dot_nt(jnp.concatenate(qs, axis=0), kw)
        ps = []
        for g in range(group):
            sink = sink_ref[kvh * group + g]
            s = jnp.where(ok, s_all[g * blk:(g + 1) * blk], NEG)
            m = jnp.maximum(jnp.max(s, axis=1, keepdims=True), sink)
            e = jnp.exp(s - m)
            den = jnp.sum(e, axis=1, keepdims=True) + jnp.exp(sink - m)
            ps.append((e / den).astype(BF16))
        o_all = _dot(jnp.concatenate(ps, axis=0), vw)
        for pr in range(group // 2):
            o_pair = jnp.where(lo, o_all[(2 * pr) * blk:(2 * pr + 1) * blk],
                               o_all[(2 * pr + 1) * blk:(2 * pr + 2) * blk])
            o_ref[j * blk:(j + 1) * blk, pr * PAIR:(pr + 1) * PAIR] = o_pair.astype(o_ref.dtype)


def _swa_attention(proj, sink_j, B, S, tq=512):
    T = B * S
    blk = SWA_BLOCK
    nsub = tq // blk
    nqt = S // tq
    nblocks = S // blk
    group = SWA_Q_HEADS // SWA_KV_HEADS
    qw = group * HEAD_DIM
    koff = C_Q // PAIR
    voff = koff + SWA_KV_HEADS

    def prev_map(off):
        return lambda b, h, i: (b * nblocks + jnp.maximum(i * nsub - 1, 0), off + h)

    def main_map(off):
        return lambda b, h, i: (b * nqt + i, off + h)

    def next_map(off):
        return lambda b, h, i: (b * nblocks + jnp.minimum(i * nsub + nsub, nblocks - 1), off + h)

    return pl.pallas_call(
        functools.partial(_swa_kernel, nblocks=nblocks),
        grid=(B, SWA_KV_HEADS, nqt),
        in_specs=[pl.BlockSpec(memory_space=pltpu.SMEM),
                  pl.BlockSpec((tq, qw), lambda b, h, i: (b * nqt + i, h)),
                  pl.BlockSpec((blk, PAIR), prev_map(koff)),
                  pl.BlockSpec((tq, PAIR), main_map(koff)),
                  pl.BlockSpec((blk, PAIR), next_map(koff)),
                  pl.BlockSpec((blk, PAIR), prev_map(voff)),
                  pl.BlockSpec((tq, PAIR), main_map(voff)),
                  pl.BlockSpec((blk, PAIR), next_map(voff))],
        out_specs=pl.BlockSpec((tq, qw), lambda b, h, i: (b * nqt + i, h)),
        out_shape=jax.ShapeDtypeStruct((T, C_Q), BF16),
        scratch_shapes=[pltpu.VMEM((tq + 2 * blk, PAIR), BF16),
                        pltpu.VMEM((tq + 2 * blk, PAIR), BF16)],
        compiler_params=_params(("parallel", "parallel", "parallel")),
        name="swa_attention",
    )(sink_j.astype(F32), proj, proj, proj, proj, proj, proj, proj)


def _extract_topk(s, k):
    nrows = s.shape[0]
    row = lax.broadcasted_iota(jnp.int32, s.shape, 0).astype(F32)
    rank = jnp.full(s.shape, float(k), F32)
    vals = []
    for r in range(k):
        m = jnp.max(s, axis=0, keepdims=True)
        first = jnp.min(jnp.where(s == m, row, float(nrows)), axis=0, keepdims=True)
        hit = row == first
        rank = jnp.where(hit, float(r), rank)
        s = jnp.where(hit, -jnp.inf, s)
        vals.append(m)
    return rank, jnp.concatenate(vals, axis=0)


def _dup_bf16_words(x):
    hi = pltpu.bitcast(x.astype(BF16).astype(F32), jnp.int32)
    return hi | lax.shift_right_logical(hi, 16)


def _peer_route_kernel(q_ref, keys_ref, n_ref, ea_ref, rb_ref, eb_ref):
    K = PEER_TOPK
    half = q_ref.shape[1] // 2
    q = q_ref[...]
    sa = _dot_nt(keys_ref[0, 0], q[:, :half])
    sb = _dot_nt(keys_ref[0, 1], q[:, half:])
    rank_a, va = _extract_topk(sa, K)
    rank_b, vb = _extract_topk(sb, K)

    ninf = jnp.full((8, sa.shape[1]), -jnp.inf, F32)
    sub = lax.broadcasted_iota(jnp.int32, ninf.shape, 0)
    pieces = [va[0:1] + vb]
    for ki in range(1, 8):
        pieces.append(jnp.where(sub < K // (ki + 1), va[ki:ki + 1] + vb[0:8], ninf))
    pieces.append(va[8:16] + vb[0:1])
    cand = jnp.concatenate(pieces, axis=0)
    rank_c, vc = _extract_topk(cand, K)
    sel = (rank_c < K).astype(F32)
    counts = [jnp.sum(sel[0:16], axis=0, keepdims=True)]
    for ki in range(1, 8):
        counts.append(jnp.sum(sel[8 + 8 * ki:16 + 8 * ki], axis=0, keepdims=True))
    counts.append(sel[72:80])
    nvec = jnp.concatenate(counts, axis=0)
    z = jnp.sum(jnp.exp(vc - vc[0:1]), axis=0, keepdims=True)

    n_of_i = jnp.zeros(sa.shape, F32)
    for ki in range(K):
        n_of_i = jnp.where(rank_a == ki, nvec[ki:ki + 1], n_of_i)
    n_ref[0] = _dup_bf16_words(n_of_i)
    ea_ref[0] = _dup_bf16_words(jnp.where(rank_a < K, jnp.exp(sa - va[0:1]) * (GELU_IN_SCALE / z), 0.0))
    rb_ref[0] = pltpu.bitcast(rank_b.astype(BF16), jnp.int32)
    eb_ref[0] = pltpu.bitcast(jnp.where(rank_b < K, jnp.exp(sb - vb[0:1]), 0.0).astype(BF16), jnp.int32)


def _peer_route(q, sub_keys, tt=256):
    T = q.shape[0]
    H = PEER_HEADS
    kd = q.shape[1] // H
    shp = jax.ShapeDtypeStruct((H, PEER_N_KEYS, T), jnp.int32)
    shp16 = jax.ShapeDtypeStruct((H, PEER_N_KEYS // 2, T), jnp.int32)
    spec = pl.BlockSpec((1, PEER_N_KEYS, tt), lambda i, h: (h, 0, i))
    spec16 = pl.BlockSpec((1, PEER_N_KEYS // 2, tt), lambda i, h: (h, 0, i))
    return pl.pallas_call(
        _peer_route_kernel,
        grid=(T // tt, H),
        in_specs=[pl.BlockSpec((tt, kd), lambda i, h: (i, h)),
                  pl.BlockSpec((1, 2, PEER_N_KEYS, kd // 2), lambda i, h: (h, 0, 0, 0))],
        out_specs=[spec, spec, spec16, spec16],
        out_shape=[shp, shp, shp16, shp16],
        compiler_params=_params(("parallel", "parallel")),
        name="peer_route",
    )(q, sub_keys)


def _peer_dense_kernel(h_ref, u_ref, vt_ref, n_ref, ea_ref, rb_ref, eb_ref, x_ref, gate_ref, g_ref, b_ref,
                       out_ref, acc_ref, pre_ref, w_ref):
    et = pl.program_id(1)

    @pl.when(et == 0)
    def _():
        acc_ref[...] = jnp.zeros(acc_ref.shape, F32)

    nk = PEER_N_KEYS
    ni = u_ref.shape[0] // nk
    nlc = h_ref.shape[0] // LANES
    pack = 16
    assert ni == 8, "the per-i rows of one expert tile are read as one aligned 8-sublane block"
    i0 = pl.multiple_of(et * ni, ni)
    half = nk // 2

    def bcast_rows(blk, ii):
        row = pltpu.bitcast(jnp.broadcast_to(blk[ii:ii + 1], (pack // 2, LANES)), BF16)
        return jnp.concatenate([row] * (half // pack), axis=0)

    pre_ref[...] = _dot_nt(u_ref[...], h_ref[...])

    def gate_block(it, carry):
        lanes = pl.ds(pl.multiple_of((it // 2) * LANES, LANES), LANES)
        jh = it % 2
        jwords = pl.ds(pl.multiple_of(jh * (half // 2), half // 2), half // 2)
        gates = [None] * ni
        for h in range(PEER_HEADS):
            n_blk = n_ref[h, pl.ds(i0, ni), lanes]
            ea_blk = ea_ref[h, pl.ds(i0, ni), lanes]
            rb = pltpu.bitcast(rb_ref[h, jwords, lanes], BF16)
            eb = pltpu.bitcast(eb_ref[h, jwords, lanes], BF16)
            for ii in range(ni):
                part = jnp.where(rb < bcast_rows(n_blk, ii), eb * bcast_rows(ea_blk, ii), 0)
                gates[ii] = part if gates[ii] is None else gates[ii] + part
        for ii in range(ni):
            rows = pl.ds(pl.multiple_of(ii * nk + jh * half, half), half)
            pre = pre_ref[rows, lanes]
            act = pre * (1.0 + lax.erf(pre))
            w_ref[rows, lanes] = gates[ii] * act.astype(BF16)
        return carry

    lax.fori_loop(0, 2 * nlc, gate_block, 0, unroll=4)
    acc_ref[...] += _dot(vt_ref[...], w_ref[...])

    @pl.when(et == pl.num_programs(1) - 1)
    def _():
        y = acc_ref[...].T
        out_ref[...] = _gated_ln(x_ref[...], y, gate_ref[0], g_ref[...], b_ref[...])


def _peer_dense(h, u, vt, routing, x, gate, g, b, S, tt=512, te=1024):
    T, D = x.shape
    E = u.shape[0]
    H, nk = PEER_HEADS, PEER_N_KEYS
    tiles_per_seq = S // tt
    rspec = pl.BlockSpec((H, nk, tt), lambda i, e: (0, 0, i))
    rspec16 = pl.BlockSpec((H, nk // 2, tt), lambda i, e: (0, 0, i))
    return pl.pallas_call(
        _peer_dense_kernel,
        grid=(T // tt, E // te),
        in_specs=[pl.BlockSpec((tt, D), lambda i, e: (i, 0)),
                  pl.BlockSpec((te, D), lambda i, e: (e, 0)),
                  pl.BlockSpec((D, te), lambda i, e: (0, e)),
                  rspec, rspec, rspec16, rspec16,
                  pl.BlockSpec((tt, D), lambda i, e: (i, 0)),
                  pl.BlockSpec((1, 1, D), lambda i, e: (i // tiles_per_seq, 0, 0)),
                  pl.BlockSpec((1, D), lambda i, e: (0, 0)),
                  pl.BlockSpec((1, D), lambda i, e: (0, 0))],
        out_specs=pl.BlockSpec((tt, D), lambda i, e: (i, 0)),
        out_shape=jax.ShapeDtypeStruct((T, D), F32),
        scratch_shapes=[pltpu.VMEM((D, tt), F32), pltpu.VMEM((te, tt), F32), pltpu.VMEM((te, tt), BF16)],
        compiler_params=_params(("parallel", "arbitrary")),
        name="peer_dense",
    )(h, u, vt, *routing, x, gate, g.reshape(1, D), b.reshape(1, D))


def _rope_tables(S):
    inv = 1.0 / (ROPE_THETA ** (jnp.arange(0, HEAD_DIM, 2, dtype=F32) / HEAD_DIM))
    ang = jnp.arange(S, dtype=F32)[:, None] * inv[None, :]
    cos, sin = jnp.cos(ang), jnp.sin(ang)
    reps = LANES // HEAD_DIM
    cs = jnp.concatenate([cos, cos] * reps, axis=1)
    sn = jnp.concatenate([-sin, sin] * reps, axis=1)
    return cs, sn


def _split3(m):
    B = m.shape[0]
    D = m.shape[1] // 3
    return (m[:, :D].reshape(B, 1, D), m[:, D:2 * D].reshape(B, 1, D), m[:, 2 * D:].reshape(B, 1, D))


def kernel(x, c, w_ada, b_ada, ln_g, ln_b, w_in_even, rpb, lam_q1, lam_k1, lam_q2, lam_k2, diff_sub_g,
           w_out_even, w_in_odd, sink, w_out_odd, peer_w_query, peer_sub_keys, peer_u, peer_v):
    B, S, D = x.shape
    T = B * S
    qscale = HEAD_DIM ** -0.5
    cs_tab, sn_tab = _rope_tables(S)
    mods = _ada_mods(c, w_ada, b_ada)
    xf = x.reshape(T, D).astype(F32)

    even_sections = ((0, A_WIDTH, False, qscale),
                     (A_WIDTH, 3 * A_WIDTH, False, 1.0),
                     (3 * A_WIDTH, 3 * A_WIDTH + B_WIDTH, True, qscale * math.log2(math.e)),
                     (3 * A_WIDTH + B_WIDTH, 3 * A_WIDTH + 2 * B_WIDTH, True, 1.0),
                     (3 * A_WIDTH + 2 * B_WIDTH, EVEN_IN, False, 1.0))
    dup = np.repeat(np.arange(SWA_KV_HEADS), 2)[:, None] * HEAD_DIM + np.arange(HEAD_DIM)[None, :]
    odd_cols = np.concatenate([np.arange(C_Q), C_Q + dup.reshape(-1), C_Q + C_KV + dup.reshape(-1)])
    odd_sections = ((0, C_Q, True, qscale),
                    (C_Q, C_Q + 2 * C_KV, True, 1.0),
                    (C_Q + 2 * C_KV, C_Q + 4 * C_KV, False, 1.0))
    peer_sections = ((0, peer_w_query.shape[-1], False, 1.0),)

    for layer in range(DEPTH):
        shift, scale, gate = _split3(mods[layer, 0])
        if layer % 2 == 0:
            i = layer // 2
            proj = _mod_proj(xf, shift, scale, w_in_even[i].astype(BF16), cs_tab, sn_tab, even_sections, S)
            o_a = _na_attention(proj, rpb[i], B, S)
            lambda_init = 0.8 - 0.6 * math.exp(-0.3 * layer)
            lam_params = jnp.stack([lam_q1[i], lam_k1[i], lam_q2[i], lam_k2[i]]).astype(F32)
            o_b = _diff_attention(proj, lam_params, diff_sub_g[i].astype(F32), lambda_init, B, S)
            xf = _proj_ln([o_a, o_b], w_out_even[i].astype(BF16), xf, gate, ln_g[layer, 0], ln_b[layer, 0], S)
        else:
            j = layer // 2
            proj = _mod_proj(xf, shift, scale, w_in_odd[j][:, odd_cols].astype(BF16), cs_tab, sn_tab,
                             odd_sections, S)
            o_c = _swa_attention(proj, sink[j], B, S)
            xf = _proj_ln([o_c], w_out_odd[j].astype(BF16), xf, gate, ln_g[layer, 0], ln_b[layer, 0], S)
        shift2, scale2, gate2 = _split3(mods[layer, 1])
        q, h2 = _mod_proj(xf, shift2, scale2, peer_w_query[layer].astype(BF16), cs_tab, sn_tab,
                          peer_sections, S, emit_h=True, h_mult=GELU_IN_SCALE)
        routing = _peer_route(q, peer_sub_keys[layer].astype(BF16))
        xf = _peer_dense(h2, peer_u[layer].astype(BF16), peer_v[layer].T.astype(BF16), routing,
                         xf, gate2, ln_g[layer, 1], ln_b[layer, 1], S)
    return xf.reshape(B, S, D).astype(x.dtype)
```

```python
import functools
import math

import numpy as np
import jax
import jax.numpy as jnp
from jax import lax
from jax.experimental import pallas as pl
from jax.experimental.pallas import tpu as pltpu

F32 = jnp.float32
BF16 = jnp.bfloat16

DEPTH = 4
GRID_W = 64
HEAD_DIM = 64
SWA_BLOCK = 128
NA_HEADS = 8
NA_WIN_ROWS = 8
NA_WIN_COLS = 16
DIFF_HEADS = 4
SWA_Q_HEADS = 16
SWA_KV_HEADS = 4
SWA_WINDOW = 128
PEER_HEADS = 8
PEER_N_KEYS = 128
PEER_TOPK = 16
ROPE_THETA = 10000.0
LN_EPS = 1e-5
NEG = -1e30
DEEPNORM_ALPHA = (2 * DEPTH) ** 0.25
GELU_IN_SCALE = math.sqrt(0.5)

A_WIDTH = NA_HEADS * HEAD_DIM
B_WIDTH = DIFF_HEADS * 2 * HEAD_DIM
EVEN_IN = 3 * A_WIDTH + 3 * B_WIDTH
C_Q = SWA_Q_HEADS * HEAD_DIM
C_KV = SWA_KV_HEADS * HEAD_DIM

LANES = 128
PAIR = 2 * HEAD_DIM
MXU_N = 256
VMEM_LIMIT = 48 * 1024 * 1024

NA_QROWS = 4
NA_KROWS = 12


def _params(sem, vmem=VMEM_LIMIT):
    return pltpu.CompilerParams(dimension_semantics=sem, vmem_limit_bytes=vmem)


def _dot(a, b):
    return jnp.dot(a, b, preferred_element_type=F32)


def _dot_nt(a, b):
    return lax.dot_general(a, b, (((1,), (1,)), ((), ())), preferred_element_type=F32)


def _lo_lanes(shape):
    return (lax.broadcasted_iota(jnp.int32, shape, len(shape) - 1) % PAIR) < HEAD_DIM


def _ada_kernel(c_ref, w_ref, b_ref, o_ref):
    cond = jax.nn.silu(c_ref[...])
    o_ref[0] = jnp.dot(cond, w_ref[0], preferred_element_type=F32,
                       precision=lax.Precision.HIGHEST) + b_ref[0]


def _ada_mods(c, w_ada, b_ada):
    B, D = c.shape
    n = w_ada.shape[0] * w_ada.shape[1]
    E = w_ada.shape[-1]
    rows = 8
    cp = jnp.zeros((rows, D), F32).at[:B].set(c.astype(F32))
    tn = 1024
    out = pl.pallas_call(
        _ada_kernel,
        grid=(n, E // tn),
        in_specs=[pl.BlockSpec((rows, D), lambda i, j: (0, 0)),
                  pl.BlockSpec((1, D, tn), lambda i, j: (i, 0, j)),
                  pl.BlockSpec((1, 1, tn), lambda i, j: (i, 0, j))],
        out_specs=pl.BlockSpec((1, rows, tn), lambda i, j: (i, 0, j)),
        out_shape=jax.ShapeDtypeStruct((n, rows, E), F32),
        compiler_params=_params(("parallel", "parallel")),
        name="ada_mods",
    )(cp, w_ada.reshape(n, D, E), b_ada.reshape(n, 1, E))
    return out[:, :B].reshape(w_ada.shape[0], w_ada.shape[1], B, E)


def _mod_proj_kernel(x_ref, shift_ref, scale_ref, w_ref, cs_ref, sn_ref, o_ref, *h_out, sections, h_mult):
    hf = x_ref[...] * (1.0 + scale_ref[0]) + shift_ref[0]
    h = hf.astype(BF16)
    if h_out:
        h_out[0][...] = (hf * h_mult).astype(BF16)
    tm = h.shape[0]
    reps = MXU_N // LANES
    for (c0, c1, rope, mult) in sections:
        for cc in range(c0, c1, MXU_N):
            acc = _dot(h, w_ref[:, cc:cc + MXU_N])
            if rope:
                cs = jnp.concatenate([cs_ref[...]] * reps, axis=1)
                sn = jnp.concatenate([sn_ref[...]] * reps, axis=1)
                first_half = (lax.broadcasted_iota(jnp.int32, (tm, MXU_N), 1) % HEAD_DIM) < HEAD_DIM // 2
                partner = jnp.where(first_half,
                                    pltpu.roll(acc, MXU_N - HEAD_DIM // 2, 1),
                                    pltpu.roll(acc, HEAD_DIM // 2, 1))
                acc = acc * cs + partner * sn
            if mult != 1.0:
                acc = acc * mult
            o_ref[:, cc:cc + MXU_N] = acc.astype(o_ref.dtype)


def _mod_proj(x, shift, scale, w, cs_tab, sn_tab, sections, S, emit_h=False, h_mult=1.0, tm=512):
    T, D = x.shape
    N = w.shape[1]
    tiles_per_seq = S // tm
    out_shape = [jax.ShapeDtypeStruct((T, N), BF16)]
    out_specs = [pl.BlockSpec((tm, N), lambda i: (i, 0))]
    if emit_h:
        out_shape.append(jax.ShapeDtypeStruct((T, D), BF16))
        out_specs.append(pl.BlockSpec((tm, D), lambda i: (i, 0)))
    res = pl.pallas_call(
        functools.partial(_mod_proj_kernel, sections=sections, h_mult=h_mult),
        grid=(T // tm,),
        in_specs=[pl.BlockSpec((tm, D), lambda i: (i, 0)),
                  pl.BlockSpec((1, 1, D), lambda i: (i // tiles_per_seq, 0, 0)),
                  pl.BlockSpec((1, 1, D), lambda i: (i // tiles_per_seq, 0, 0)),
                  pl.BlockSpec((D, N), lambda i: (0, 0)),
                  pl.BlockSpec((tm, LANES), lambda i: (i % tiles_per_seq, 0)),
                  pl.BlockSpec((tm, LANES), lambda i: (i % tiles_per_seq, 0))],
        out_specs=out_specs,
        out_shape=out_shape,
        compiler_params=_params(("parallel",)),
        name="mod_proj",
    )(x, shift, scale, w, cs_tab, sn_tab)
    return res if emit_h else res[0]


def _gated_ln(x, y, gate, g, b):
    z = DEEPNORM_ALPHA * x + gate * y
    mu = jnp.mean(z, axis=-1, keepdims=True)
    zc = z - mu
    var = jnp.mean(zc * zc, axis=-1, keepdims=True)
    return zc * lax.rsqrt(var + LN_EPS) * g + b


def _proj_ln_kernel(*refs, widths):
    n = len(widths)
    o_refs = refs[:n]
    w_ref, x_ref, gate_ref, g_ref, b_ref, out_ref = refs[n:]
    y = None
    k0 = 0
    for o_ref, wd in zip(o_refs, widths):
        part = _dot(o_ref[...], w_ref[k0:k0 + wd, :])
        y = part if y is None else y + part
        k0 += wd
    out_ref[...] = _gated_ln(x_ref[...], y, gate_ref[0], g_ref[...], b_ref[...])


def _proj_ln(o_list, w, x, gate, g, b, S, tm=512):
    T, D = x.shape
    widths = tuple(o.shape[1] for o in o_list)
    tiles_per_seq = S // tm
    in_specs = [pl.BlockSpec((tm, wd), lambda i: (i, 0)) for wd in widths]
    in_specs += [pl.BlockSpec((sum(widths), D), lambda i: (0, 0)),
                 pl.BlockSpec((tm, D), lambda i: (i, 0)),
                 pl.BlockSpec((1, 1, D), lambda i: (i // tiles_per_seq, 0, 0)),
                 pl.BlockSpec((1, D), lambda i: (0, 0)),
                 pl.BlockSpec((1, D), lambda i: (0, 0))]
    return pl.pallas_call(
        functools.partial(_proj_ln_kernel, widths=widths),
        grid=(T // tm,),
        in_specs=in_specs,
        out_specs=pl.BlockSpec((tm, D), lambda i: (i, 0)),
        out_shape=jax.ShapeDtypeStruct((T, D), F32),
        compiler_params=_params(("parallel",)),
        name="proj_ln",
    )(*o_list, w, x, gate, g.reshape(1, D), b.reshape(1, D))


def _na_kernel(q_ref, k_ref, v_ref, bias_ref, o_ref, *, grid_rows):
    blk = pl.program_id(2)
    start_row = jnp.clip(NA_QROWS * blk - NA_WIN_ROWS // 2, 0, grid_rows - NA_KROWS)
    start = pl.multiple_of(start_row * GRID_W, GRID_W)
    nk = NA_KROWS * GRID_W
    kw = k_ref[pl.ds(start, nk), :]
    vw = v_ref[pl.ds(start, nk), :]
    q = q_ref[...]
    lo_q = _lo_lanes(q.shape)
    lo_v = _lo_lanes(vw.shape)
    out = None
    for c in range(2):
        qm = jnp.where(lo_q if c == 0 else ~lo_q, q, jnp.zeros_like(q))
        s = _dot_nt(qm, kw) + bias_ref[0, c]
        m = jnp.max(s, axis=1, keepdims=True)
        e = jnp.exp(s - m)
        p = (e / jnp.sum(e, axis=1, keepdims=True)).astype(BF16)
        vm = jnp.where(lo_v if c == 0 else ~lo_v, vw, jnp.zeros_like(vw))
        part = _dot(p, vm)
        out = part if out is None else out + part
    o_ref[...] = out.astype(o_ref.dtype)


def _na_bias_tables(rpb_i, grid_rows):
    nblk = grid_rows // NA_QROWS
    ndr, ndc = 2 * NA_WIN_ROWS - 1, 2 * NA_WIN_COLS - 1
    col = np.arange(GRID_W)
    col_start = np.clip(col - NA_WIN_COLS // 2, 0, GRID_W - NA_WIN_COLS)
    col_ok = (col[None, :] >= col_start[:, None]) & (col[None, :] < col_start[:, None] + NA_WIN_COLS)
    dc = np.clip(col[None, :] - col[:, None] + NA_WIN_COLS - 1, 0, ndc - 1)
    pick = (dc[None] == np.arange(ndc)[:, None, None]).astype(np.float32)
    blocks = jnp.einsum('hrd,dqk->hrqk', rpb_i.astype(F32), jnp.asarray(pick), precision=lax.Precision.HIGHEST)
    blocks = jnp.where(jnp.asarray(col_ok)[None, None], blocks, NEG)
    blocks = jnp.concatenate([blocks, jnp.full((NA_HEADS, 1, GRID_W, GRID_W), NEG, F32)], axis=1)
    tabs = []
    for blk in (0, 1, nblk - 1):
        start_row = int(np.clip(NA_QROWS * blk - NA_WIN_ROWS // 2, 0, grid_rows - NA_KROWS))
        qr = NA_QROWS * blk + np.arange(NA_QROWS)
        kr = start_row + np.arange(NA_KROWS)
        row_start = np.clip(qr - NA_WIN_ROWS // 2, 0, grid_rows - NA_WIN_ROWS)
        row_ok = (kr[None, :] >= row_start[:, None]) & (kr[None, :] < row_start[:, None] + NA_WIN_ROWS)
        dr = np.where(row_ok, kr[None, :] - qr[:, None] + NA_WIN_ROWS - 1, ndr)
        tab = jnp.concatenate(
            [jnp.concatenate([blocks[:, int(dr[a, b])] for b in range(NA_KROWS)], axis=-1)
             for a in range(NA_QROWS)], axis=-2)
        tabs.append(tab)
    return jnp.stack(tabs)


def _na_attention(proj, rpb_i, B, S):
    T = B * S
    grid_rows = S // GRID_W
    nblk = grid_rows // NA_QROWS
    nq, nk = NA_QROWS * GRID_W, NA_KROWS * GRID_W
    npairs = A_WIDTH // PAIR
    bias = _na_bias_tables(rpb_i, grid_rows)

    def cls(j):
        return jnp.where(j == 0, 0, jnp.where(j == nblk - 1, 2, 1))

    return pl.pallas_call(
        functools.partial(_na_kernel, grid_rows=grid_rows),
        grid=(B, npairs, nblk),
        in_specs=[pl.BlockSpec((nq, PAIR), lambda b, p, j: (b * nblk + j, p)),
                  pl.BlockSpec((S, PAIR), lambda b, p, j: (b, npairs + p)),
                  pl.BlockSpec((S, PAIR), lambda b, p, j: (b, 2 * npairs + p)),
                  pl.BlockSpec((1, 2, nq, nk), lambda b, p, j: (cls(j), p, 0, 0))],
        out_specs=pl.BlockSpec((nq, PAIR), lambda b, p, j: (b * nblk + j, p)),
        out_shape=jax.ShapeDtypeStruct((T, A_WIDTH), BF16),
        compiler_params=_params(("parallel", "parallel", "arbitrary")),
        name="na_attention",
    )(proj, proj, proj, bias)


def _diff_kernel(q_ref, k_ref, v_ref, lamp_ref, subg_ref, o_ref, m_ref, l_ref, acc_ref, *, lambda_init):
    kv = pl.program_id(3)

    @pl.when(kv == 0)
    def _():
        m_ref[...] = jnp.full(m_ref.shape, -jnp.inf, F32)
        l_ref[...] = jnp.zeros(l_ref.shape, F32)
        acc_ref[...] = jnp.zeros(acc_ref.shape, F32)

    q = q_ref[...]
    k = k_ref[...]
    v = v_ref[...]
    lo = _lo_lanes(q.shape)
    reps = k.shape[0] // LANES
    for c in range(2):
        qm = jnp.where(lo if c == 0 else ~lo, q, jnp.zeros_like(q))
        s = _dot_nt(qm, k)
        m_prev = m_ref[c]
        m_new = jnp.maximum(m_prev, jnp.max(s, axis=1, keepdims=True))
        alpha = jnp.exp2(m_prev - m_new)
        p = jnp.exp2(s - jnp.concatenate([m_new] * reps, axis=1))
        l_ref[c] = alpha * l_ref[c] + jnp.sum(p, axis=1, keepdims=True)
        acc_ref[c] = alpha * acc_ref[c] + _dot(p.astype(BF16), v)
        m_ref[c] = m_new

    @pl.when(kv == pl.num_programs(3) - 1)
    def _():
        lp = lamp_ref[...]
        lam = (jnp.exp(jnp.sum(lp[0:1] * lp[1:2], axis=1, keepdims=True))
               - jnp.exp(jnp.sum(lp[2:3] * lp[3:4], axis=1, keepdims=True)) + lambda_init)
        o = acc_ref[0] / l_ref[0] - lam * (acc_ref[1] / l_ref[1])
        o = o * lax.rsqrt(jnp.mean(o * o, axis=-1, keepdims=True) + LN_EPS) * subg_ref[...]
        o_ref[...] = (o * (1.0 - lambda_init)).astype(o_ref.dtype)


def _diff_attention(proj, lam_params, sub_g, lambda_init, B, S, tq=1024, tk=1024):
    T = B * S
    off = 3 * A_WIDTH // PAIR
    nh = DIFF_HEADS
    nq, nkv = S // tq, S // tk
    return pl.pallas_call(
        functools.partial(_diff_kernel, lambda_init=lambda_init),
        grid=(B, nh, nq, nkv),
        in_specs=[pl.BlockSpec((tq, PAIR), lambda b, h, i, j: (b * nq + i, off + h)),
                  pl.BlockSpec((tk, PAIR), lambda b, h, i, j: (b * nkv + j, off + nh + h)),
                  pl.BlockSpec((tk, PAIR), lambda b, h, i, j: (b * nkv + j, off + 2 * nh + h)),
                  pl.BlockSpec((4, HEAD_DIM), lambda b, h, i, j: (0, 0)),
                  pl.BlockSpec((1, PAIR), lambda b, h, i, j: (0, 0))],
        out_specs=pl.BlockSpec((tq, PAIR), lambda b, h, i, j: (b * nq + i, h)),
        out_shape=jax.ShapeDtypeStruct((T, B_WIDTH), BF16),
        scratch_shapes=[pltpu.VMEM((2, tq, LANES), F32),
                        pltpu.VMEM((2, tq, LANES), F32),
                        pltpu.VMEM((2, tq, PAIR), F32)],
        compiler_params=_params(("parallel", "parallel", "parallel", "arbitrary")),
        name="diff_attention",
    )(proj, proj, proj, lam_params, sub_g.reshape(1, PAIR))


def _swa_kernel(sink_ref, q_ref, kp_ref, km_ref, kn_ref, vp_ref, vm_ref, vn_ref, o_ref, kbuf, vbuf, *, nblocks):
    kvh = pl.program_id(1)
    qt = pl.program_id(2)
    blk = SWA_BLOCK
    nsub = q_ref.shape[0] // blk
    group = SWA_Q_HEADS // SWA_KV_HEADS
    kbuf[0:blk] = kp_ref[...]
    kbuf[blk:blk + nsub * blk] = km_ref[...]
    kbuf[blk + nsub * blk:] = kn_ref[...]
    vbuf[0:blk] = vp_ref[...]
    vbuf[blk:blk + nsub * blk] = vm_ref[...]
    vbuf[blk + nsub * blk:] = vn_ref[...]

    qi = lax.broadcasted_iota(jnp.int32, (blk, 3 * blk), 0)
    kk = lax.broadcasted_iota(jnp.int32, (blk, 3 * blk), 1)
    band = (kk - qi >= blk - SWA_WINDOW) & (kk - qi <= blk + SWA_WINDOW)
    lo = _lo_lanes((blk, PAIR))
    for j in range(nsub):
        n = qt * nsub + j
        first_key = jnp.where(n > 0, 0, blk)
        end_key = jnp.where(n < nblocks - 1, 3 * blk, 2 * blk)
        ok = band & (kk >= first_key) & (kk < end_key)
        kw = kbuf[j * blk:(j + 3) * blk]
        vw = vbuf[j * blk:(j + 3) * blk]
        qs = []
        for pr in range(group // 2):
            qp = q_ref[j * blk:(j + 1) * blk, pr * PAIR:(pr + 1) * PAIR]
            qs.append(jnp.where(lo, qp, jnp.zeros_like(qp)))
            qs.append(jnp.where(lo, jnp.zeros_like(qp), qp))
        s_all = _dot_nt(jnp.concatenate(qs, axis=0), kw)
        ps = []
        for g in range(group):
            sink = sink_ref[kvh * group + g]
            s = jnp.where(ok, s_all[g * blk:(g + 1) * blk], NEG)
            m = jnp.maximum(jnp.max(s, axis=1, keepdims=True), sink)
            e = jnp.exp(s - m)
            den = jnp.sum(e, axis=1, keepdims=True) + jnp.exp(sink - m)
            ps.append((e / den).astype(BF16))
        o_all = _dot(jnp.concatenate(ps, axis=0), vw)
        for pr in range(group // 2):
            o_pair = jnp.where(lo, o_all[(2 * pr) * blk:(2 * pr + 1) * blk],
                               o_all[(2 * pr + 1) * blk:(2 * pr + 2) * blk])
            o_ref[j * blk:(j + 1) * blk, pr * PAIR:(pr + 1) * PAIR] = o_pair.astype(o_ref.dtype)


def _swa_attention(proj, sink_j, B, S, tq=512):
    T = B * S
    blk = SWA_BLOCK
    nsub = tq // blk
    nqt = S // tq
    nblocks = S // blk
    group = SWA_Q_HEADS // SWA_KV_HEADS
    qw = group * HEAD_DIM
    koff = C_Q // PAIR
    voff = koff + SWA_KV_HEADS

    def prev_map(off):
        return lambda b, h, i: (b * nblocks + jnp.maximum(i * nsub - 1, 0), off + h)

    def main_map(off):
        return lambda b, h, i: (b * nqt + i, off + h)

    def next_map(off):
        return lambda b, h, i: (b * nblocks + jnp.minimum(i * nsub + nsub, nblocks - 1), off + h)

    return pl.pallas_call(
        functools.partial(_swa_kernel, nblocks=nblocks),
        grid=(B, SWA_KV_HEADS, nqt),
        in_specs=[pl.BlockSpec(memory_space=pltpu.SMEM),
                  pl.BlockSpec((tq, qw), lambda b, h, i: (b * nqt + i, h)),
                  pl.BlockSpec((blk, PAIR), prev_map(koff)),
                  pl.BlockSpec((tq, PAIR), main_map(koff)),
                  pl.BlockSpec((blk, PAIR), next_map(koff)),
                  pl.BlockSpec((blk, PAIR), prev_map(voff)),
                  pl.BlockSpec((tq, PAIR), main_map(voff)),
                  pl.BlockSpec((blk, PAIR), next_map(voff))],
        out_specs=pl.BlockSpec((tq, qw), lambda b, h, i: (b * nqt + i, h)),
        out_shape=jax.ShapeDtypeStruct((T, C_Q), BF16),
        scratch_shapes=[pltpu.VMEM((tq + 2 * blk, PAIR), BF16),
                        pltpu.VMEM((tq + 2 * blk, PAIR), BF16)],
        compiler_params=_params(("parallel", "parallel", "parallel")),
        name="swa_attention",
    )(sink_j.astype(F32), proj, proj, proj, proj, proj, proj, proj)


def _extract_topk(s, k, want_rank=True):
    nrows = s.shape[0]
    row = lax.broadcasted_iota(jnp.int32, s.shape, 0).astype(F32)
    rank = jnp.full(s.shape, float(k), F32) if want_rank else None
    vals, firsts = [], []
    for r in range(k):
        m = jnp.max(s, axis=0, keepdims=True)
        first = jnp.min(jnp.where(s == m, row, float(nrows)), axis=0, keepdims=True)
        hit = row == first
        if want_rank:
            rank = jnp.where(hit, float(r), rank)
        s = jnp.where(hit, -jnp.inf, s)
        vals.append(m)
        firsts.append(first)
    return rank, jnp.concatenate(vals, axis=0), jnp.concatenate(firsts, axis=0)


def _dup_bf16_words(x):
    hi = pltpu.bitcast(x.astype(BF16).astype(F32), jnp.int32)
    return hi | lax.shift_right_logical(hi, 16)


def _peer_route_kernel(q_ref, keys_ref, n_ref, ea_ref, rb_ref, eb_ref):
    K = PEER_TOPK
    half = q_ref.shape[1] // 2
    q = q_ref[...]
    sa = _dot_nt(keys_ref[0, 0], q[:, :half])
    sb = _dot_nt(keys_ref[0, 1], q[:, half:])
    _, va, rows_a = _extract_topk(sa, K, want_rank=False)
    rank_b, vb, _ = _extract_topk(sb, K)

    ninf = jnp.full((8, sa.shape[1]), -jnp.inf, F32)
    sub = lax.broadcasted_iota(jnp.int32, ninf.shape, 0)
    pieces = [va[0:1] + vb]
    for ki in range(1, 8):
        pieces.append(jnp.where(sub < K // (ki + 1), va[ki:ki + 1] + vb[0:8], ninf))
    pieces.append(va[8:16] + vb[0:1])
    cand = jnp.concatenate(pieces, axis=0)
    rank_c, vc, _ = _extract_topk(cand, K)
    sel = (rank_c < K).astype(F32)
    counts = [jnp.sum(sel[0:16], axis=0, keepdims=True)]
    for ki in range(1, 8):
        counts.append(jnp.sum(sel[8 + 8 * ki:16 + 8 * ki], axis=0, keepdims=True))
    counts.append(sel[72:80])
    nvec = jnp.concatenate(counts, axis=0)
    z = jnp.sum(jnp.exp(vc - vc[0:1]), axis=0, keepdims=True)

    row = lax.broadcasted_iota(jnp.int32, sa.shape, 0).astype(F32)
    n_of_i = jnp.zeros(sa.shape, F32)
    for ki in range(K):
        n_of_i = jnp.where(row == rows_a[ki:ki + 1], nvec[ki:ki + 1], n_of_i)
    n_ref[0] = _dup_bf16_words(n_of_i)
    ea_ref[0] = _dup_bf16_words(jnp.exp(sa - va[0:1]) * (GELU_IN_SCALE / z))
    rb_ref[0] = pltpu.bitcast(rank_b.astype(BF16), jnp.int32)
    eb_ref[0] = pltpu.bitcast(jnp.exp(sb - vb[0:1]).astype(BF16), jnp.int32)


def _peer_route(q, sub_keys, tt=256):
    T = q.shape[0]
    H = PEER_HEADS
    kd = q.shape[1] // H
    shp = jax.ShapeDtypeStruct((H, PEER_N_KEYS, T), jnp.int32)
    shp16 = jax.ShapeDtypeStruct((H, PEER_N_KEYS // 2, T), jnp.int32)
    spec = pl.BlockSpec((1, PEER_N_KEYS, tt), lambda i, h: (h, 0, i))
    spec16 = pl.BlockSpec((1, PEER_N_KEYS // 2, tt), lambda i, h: (h, 0, i))
    return pl.pallas_call(
        _peer_route_kernel,
        grid=(T // tt, H),
        in_specs=[pl.BlockSpec((tt, kd), lambda i, h: (i, h)),
                  pl.BlockSpec((1, 2, PEER_N_KEYS, kd // 2), lambda i, h: (h, 0, 0, 0))],
        out_specs=[spec, spec, spec16, spec16],
        out_shape=[shp, shp, shp16, shp16],
        compiler_params=_params(("parallel", "parallel")),
        name="peer_route",
    )(q, sub_keys)


def _peer_dense_kernel(h_ref, u_ref, vt_ref, n_ref, ea_ref, rb_ref, eb_ref, x_ref, gate_ref, g_ref, b_ref,
                       out_ref, acc_ref, pre_ref, w_ref):
    et = pl.program_id(1)

    @pl.when(et == 0)
    def _():
        acc_ref[...] = jnp.zeros(acc_ref.shape, F32)

    nk = PEER_N_KEYS
    ni = u_ref.shape[0] // nk
    nlc = h_ref.shape[0] // LANES
    pack = 16
    assert ni == 8, "the per-i rows of one expert tile are read as one aligned 8-sublane block"
    i0 = pl.multiple_of(et * ni, ni)
    half = nk // 2
    igroup = 2

    def bcast_rows(blk, ii):
        row = pltpu.bitcast(jnp.broadcast_to(blk[ii:ii + 1], (pack // 2, LANES)), BF16)
        return jnp.concatenate([row] * (half // pack), axis=0)

    pre_ref[...] = _dot_nt(u_ref[...], h_ref[...])

    def gate_block(it, carry):
        lanes = pl.ds(pl.multiple_of((it // 2) * LANES, LANES), LANES)
        jh = it % 2
        jwords = pl.ds(pl.multiple_of(jh * (half // 2), half // 2), half // 2)
        for ig in range(0, ni, igroup):
            gates = [None] * igroup
            for h in range(PEER_HEADS):
                n_blk = n_ref[h, pl.ds(i0, ni), lanes]
                ea_blk = ea_ref[h, pl.ds(i0, ni), lanes]
                rb = pltpu.bitcast(rb_ref[h, jwords, lanes], BF16)
                eb = pltpu.bitcast(eb_ref[h, jwords, lanes], BF16)
                for k in range(igroup):
                    part = jnp.where(rb < bcast_rows(n_blk, ig + k), eb * bcast_rows(ea_blk, ig + k), 0)
                    gates[k] = part if gates[k] is None else gates[k] + part
            for k in range(igroup):
                rows = pl.ds(pl.multiple_of((ig + k) * nk + jh * half, half), half)
                pre = pre_ref[rows, lanes]
                act = pre * (1.0 + lax.erf(pre))
                w_ref[rows, lanes] = gates[k] * act.astype(BF16)
        return carry

    lax.fori_loop(0, 2 * nlc, gate_block, 0, unroll=8)
    acc_ref[...] += _dot(vt_ref[...], w_ref[...])

    @pl.when(et == pl.num_programs(1) - 1)
    def _():
        y = acc_ref[...].T
        out_ref[...] = _gated_ln(x_ref[...], y, gate_ref[0], g_ref[...], b_ref[...])


def _peer_dense(h, u, vt, routing, x, gate, g, b, S, tt=512, te=1024):
    T, D = x.shape
    E = u.shape[0]
    H, nk = PEER_HEADS, PEER_N_KEYS
    tiles_per_seq = S // tt
    rspec = pl.BlockSpec((H, nk, tt), lambda i, e: (0, 0, i))
    rspec16 = pl.BlockSpec((H, nk // 2, tt), lambda i, e: (0, 0, i))
    return pl.pallas_call(
        _peer_dense_kernel,
        grid=(T // tt, E // te),
        in_specs=[pl.BlockSpec((tt, D), lambda i, e: (i, 0)),
                  pl.BlockSpec((te, D), lambda i, e: (e, 0)),
                  pl.BlockSpec((D, te), lambda i, e: (0, e)),
                  rspec, rspec, rspec16, rspec16,
                  pl.BlockSpec((tt, D), lambda i, e: (i, 0)),
                  pl.BlockSpec((1, 1, D), lambda i, e: (i // tiles_per_seq, 0, 0)),
                  pl.BlockSpec((1, D), lambda i, e: (0, 0)),
                  pl.BlockSpec((1, D), lambda i, e: (0, 0))],
        out_specs=pl.BlockSpec((tt, D), lambda i, e: (i, 0)),
        out_shape=jax.ShapeDtypeStruct((T, D), F32),
        scratch_shapes=[pltpu.VMEM((D, tt), F32), pltpu.VMEM((te, tt), F32), pltpu.VMEM((te, tt), BF16)],
        compiler_params=_params(("parallel", "arbitrary")),
        name="peer_dense",
    )(h, u, vt, *routing, x, gate, g.reshape(1, D), b.reshape(1, D))


def _rope_tables(S):
    inv = 1.0 / (ROPE_THETA ** (jnp.arange(0, HEAD_DIM, 2, dtype=F32) / HEAD_DIM))
    ang = jnp.arange(S, dtype=F32)[:, None] * inv[None, :]
    cos, sin = jnp.cos(ang), jnp.sin(ang)
    reps = LANES // HEAD_DIM
    cs = jnp.concatenate([cos, cos] * reps, axis=1)
    sn = jnp.concatenate([-sin, sin] * reps, axis=1)
    return cs, sn


def _split3(m):
    B = m.shape[0]
    D = m.shape[1] // 3
    return (m[:, :D].reshape(B, 1, D), m[:, D:2 * D].reshape(B, 1, D), m[:, 2 * D:].reshape(B, 1, D))


def kernel(x, c, w_ada, b_ada, ln_g, ln_b, w_in_even, rpb, lam_q1, lam_k1, lam_q2, lam_k2, diff_sub_g,
           w_out_even, w_in_odd, sink, w_out_odd, peer_w_query, peer_sub_keys, peer_u, peer_v):
    B, S, D = x.shape
    T = B * S
    qscale = HEAD_DIM ** -0.5
    cs_tab, sn_tab = _rope_tables(S)
    mods = _ada_mods(c, w_ada, b_ada)
    xf = x.reshape(T, D).astype(F32)

    even_sections = ((0, A_WIDTH, False, qscale),
                     (A_WIDTH, 3 * A_WIDTH, False, 1.0),
                     (3 * A_WIDTH, 3 * A_WIDTH + B_WIDTH, True, qscale * math.log2(math.e)),
                     (3 * A_WIDTH + B_WIDTH, 3 * A_WIDTH + 2 * B_WIDTH, True, 1.0),
                     (3 * A_WIDTH + 2 * B_WIDTH, EVEN_IN, False, 1.0))
    dup = np.repeat(np.arange(SWA_KV_HEADS), 2)[:, None] * HEAD_DIM + np.arange(HEAD_DIM)[None, :]
    odd_cols = np.concatenate([np.arange(C_Q), C_Q + dup.reshape(-1), C_Q + C_KV + dup.reshape(-1)])
    odd_sections = ((0, C_Q, True, qscale),
                    (C_Q, C_Q + 2 * C_KV, True, 1.0),
                    (C_Q + 2 * C_KV, C_Q + 4 * C_KV, False, 1.0))
    peer_sections = ((0, peer_w_query.shape[-1], False, 1.0),)

    for layer in range(DEPTH):
        shift, scale, gate = _split3(mods[layer, 0])
        if layer % 2 == 0:
            i = layer // 2
            proj = _mod_proj(xf, shift, scale, w_in_even[i].astype(BF16), cs_tab, sn_tab, even_sections, S)
            o_a = _na_attention(proj, rpb[i], B, S)
            lambda_init = 0.8 - 0.6 * math.exp(-0.3 * layer)
            lam_params = jnp.stack([lam_q1[i], lam_k1[i], lam_q2[i], lam_k2[i]]).astype(F32)
            o_b = _diff_attention(proj, lam_params, diff_sub_g[i].astype(F32), lambda_init, B, S)
            xf = _proj_ln([o_a, o_b], w_out_even[i].astype(BF16), xf, gate, ln_g[layer, 0], ln_b[layer, 0], S)
        else:
            j = layer // 2
            proj = _mod_proj(xf, shift, scale, w_in_odd[j][:, odd_cols].astype(BF16), cs_tab, sn_tab,
                             odd_sections, S)
            o_c = _swa_attention(proj, sink[j], B, S)
            xf = _proj_ln([o_c], w_out_odd[j].astype(BF16), xf, gate, ln_g[layer, 0], ln_b[layer, 0], S)
        shift2, scale2, gate2 = _split3(mods[layer, 1])
        q, h2 = _mod_proj(xf, shift2, scale2, peer_w_query[layer].astype(BF16), cs_tab, sn_tab,
                          peer_sections, S, emit_h=True, h_mult=GELU_IN_SCALE)
        routing = _peer_route(q, peer_sub_keys[layer].astype(BF16))
        xf = _peer_dense(h2, peer_u[layer].astype(BF16), peer_v[layer].T.astype(BF16), routing,
                         xf, gate2, ln_g[layer, 1], ln_b[layer, 1], S)
    return xf.reshape(B, S, D).astype(x.dtype)
```

```python
import functools
import math

import numpy as np
import jax
import jax.numpy as jnp
from jax import lax
from jax.experimental import pallas as pl
from jax.experimental.pallas import tpu as pltpu

F32 = jnp.float32
BF16 = jnp.bfloat16

DEPTH = 4
GRID_W = 64
HEAD_DIM = 64
SWA_BLOCK = 128
NA_HEADS = 8
NA_WIN_ROWS = 8
NA_WIN_COLS = 16
DIFF_HEADS = 4
SWA_Q_HEADS = 16
SWA_KV_HEADS = 4
SWA_WINDOW = 128
PEER_HEADS = 8
PEER_N_KEYS = 128
PEER_TOPK = 16
ROPE_THETA = 10000.0
LN_EPS = 1e-5
NEG = -1e30
DEEPNORM_ALPHA = (2 * DEPTH) ** 0.25
GELU_IN_SCALE = math.sqrt(0.5)

A_WIDTH = NA_HEADS * HEAD_DIM
B_WIDTH = DIFF_HEADS * 2 * HEAD_DIM
EVEN_IN = 3 * A_WIDTH + 3 * B_WIDTH
C_Q = SWA_Q_HEADS * HEAD_DIM
C_KV = SWA_KV_HEADS * HEAD_DIM

LANES = 128
PAIR = 2 * HEAD_DIM
MXU_N = 256
VMEM_LIMIT = 48 * 1024 * 1024

NA_QROWS = 4
NA_KROWS = 12
NA_PAIRS_PER_STEP = 2


def _params(sem, vmem=VMEM_LIMIT):
    return pltpu.CompilerParams(dimension_semantics=sem, vmem_limit_bytes=vmem)


def _dot(a, b):
    return jnp.dot(a, b, preferred_element_type=F32)


def _dot_nt(a, b):
    return lax.dot_general(a, b, (((1,), (1,)), ((), ())), preferred_element_type=F32)


def _lo_lanes(shape):
    return (lax.broadcasted_iota(jnp.int32, shape, len(shape) - 1) % PAIR) < HEAD_DIM


def _ada_kernel(c_ref, w_ref, b_ref, o_ref):
    cond = jax.nn.silu(c_ref[...])
    o_ref[0] = jnp.dot(cond, w_ref[0], preferred_element_type=F32,
                       precision=lax.Precision.HIGHEST) + b_ref[0]


def _ada_mods(c, w_ada, b_ada):
    B, D = c.shape
    n = w_ada.shape[0] * w_ada.shape[1]
    E = w_ada.shape[-1]
    rows = 8
    cp = jnp.zeros((rows, D), F32).at[:B].set(c.astype(F32))
    tn = 1024
    out = pl.pallas_call(
        _ada_kernel,
        grid=(n, E // tn),
        in_specs=[pl.BlockSpec((rows, D), lambda i, j: (0, 0)),
                  pl.BlockSpec((1, D, tn), lambda i, j: (i, 0, j)),
                  pl.BlockSpec((1, 1, tn), lambda i, j: (i, 0, j))],
        out_specs=pl.BlockSpec((1, rows, tn), lambda i, j: (i, 0, j)),
        out_shape=jax.ShapeDtypeStruct((n, rows, E), F32),
        compiler_params=_params(("parallel", "parallel")),
        name="ada_mods",
    )(cp, w_ada.reshape(n, D, E), b_ada.reshape(n, 1, E))
    return out[:, :B].reshape(w_ada.shape[0], w_ada.shape[1], B, E)


def _mod_proj_kernel(x_ref, shift_ref, scale_ref, w_ref, cs_ref, sn_ref, o_ref, *h_out, sections, h_mult):
    hf = x_ref[...] * (1.0 + scale_ref[0]) + shift_ref[0]
    h = hf.astype(BF16)
    if h_out:
        h_out[0][...] = (hf * h_mult).astype(BF16)
    tm = h.shape[0]
    reps = MXU_N // LANES
    for (c0, c1, rope, mult) in sections:
        for cc in range(c0, c1, MXU_N):
            acc = _dot(h, w_ref[:, cc:cc + MXU_N])
            if rope:
                cs = jnp.concatenate([cs_ref[...]] * reps, axis=1)
                sn = jnp.concatenate([sn_ref[...]] * reps, axis=1)
                first_half = (lax.broadcasted_iota(jnp.int32, (tm, MXU_N), 1) % HEAD_DIM) < HEAD_DIM // 2
                partner = jnp.where(first_half,
                                    pltpu.roll(acc, MXU_N - HEAD_DIM // 2, 1),
                                    pltpu.roll(acc, HEAD_DIM // 2, 1))
                acc = acc * cs + partner * sn
            if mult != 1.0:
                acc = acc * mult
            o_ref[:, cc:cc + MXU_N] = acc.astype(o_ref.dtype)


def _mod_proj(x, shift, scale, w, cs_tab, sn_tab, sections, S, emit_h=False, h_mult=1.0, tm=512):
    T, D = x.shape
    N = w.shape[1]
    tiles_per_seq = S // tm
    out_shape = [jax.ShapeDtypeStruct((T, N), BF16)]
    out_specs = [pl.BlockSpec((tm, N), lambda i: (i, 0))]
    if emit_h:
        out_shape.append(jax.ShapeDtypeStruct((T, D), BF16))
        out_specs.append(pl.BlockSpec((tm, D), lambda i: (i, 0)))
    res = pl.pallas_call(
        functools.partial(_mod_proj_kernel, sections=sections, h_mult=h_mult),
        grid=(T // tm,),
        in_specs=[pl.BlockSpec((tm, D), lambda i: (i, 0)),
                  pl.BlockSpec((1, 1, D), lambda i: (i // tiles_per_seq, 0, 0)),
                  pl.BlockSpec((1, 1, D), lambda i: (i // tiles_per_seq, 0, 0)),
                  pl.BlockSpec((D, N), lambda i: (0, 0)),
                  pl.BlockSpec((tm, LANES), lambda i: (i % tiles_per_seq, 0)),
                  pl.BlockSpec((tm, LANES), lambda i: (i % tiles_per_seq, 0))],
        out_specs=out_specs,
        out_shape=out_shape,
        compiler_params=_params(("parallel",)),
        name="mod_proj",
    )(x, shift, scale, w, cs_tab, sn_tab)
    return res if emit_h else res[0]


def _gated_ln(x, y, gate, g, b):
    z = DEEPNORM_ALPHA * x + gate * y
    mu = jnp.mean(z, axis=-1, keepdims=True)
    zc = z - mu
    var = jnp.mean(zc * zc, axis=-1, keepdims=True)
    return zc * lax.rsqrt(var + LN_EPS) * g + b


def _proj_ln_kernel(*refs, widths):
    n = len(widths)
    o_refs = refs[:n]
    w_ref, x_ref, gate_ref, g_ref, b_ref, out_ref = refs[n:]
    y = None
    k0 = 0
    for o_ref, wd in zip(o_refs, widths):
        part = _dot(o_ref[...], w_ref[k0:k0 + wd, :])
        y = part if y is None else y + part
        k0 += wd
    out_ref[...] = _gated_ln(x_ref[...], y, gate_ref[0], g_ref[...], b_ref[...])


def _proj_ln(o_list, w, x, gate, g, b, S, tm=512):
    T, D = x.shape
    widths = tuple(o.shape[1] for o in o_list)
    tiles_per_seq = S // tm
    in_specs = [pl.BlockSpec((tm, wd), lambda i: (i, 0)) for wd in widths]
    in_specs += [pl.BlockSpec((sum(widths), D), lambda i: (0, 0)),
                 pl.BlockSpec((tm, D), lambda i: (i, 0)),
                 pl.BlockSpec((1, 1, D), lambda i: (i // tiles_per_seq, 0, 0)),
                 pl.BlockSpec((1, D), lambda i: (0, 0)),
                 pl.BlockSpec((1, D), lambda i: (0, 0))]
    return pl.pallas_call(
        functools.partial(_proj_ln_kernel, widths=widths),
        grid=(T // tm,),
        in_specs=in_specs,
        out_specs=pl.BlockSpec((tm, D), lambda i: (i, 0)),
        out_shape=jax.ShapeDtypeStruct((T, D), F32),
        compiler_params=_params(("parallel",)),
        name="proj_ln",
    )(*o_list, w, x, gate, g.reshape(1, D), b.reshape(1, D))


def _na_kernel(q_ref, k_ref, v_ref, bias_ref, o_ref, *, grid_rows):
    blk = pl.program_id(2)
    start_row = jnp.clip(NA_QROWS * blk - NA_WIN_ROWS // 2, 0, grid_rows - NA_KROWS)
    start = pl.multiple_of(start_row * GRID_W, GRID_W)
    nk = NA_KROWS * GRID_W
    lo_q = _lo_lanes((q_ref.shape[0], PAIR))
    lo_v = _lo_lanes((nk, PAIR))
    for pr in range(q_ref.shape[1] // PAIR):
        cols = slice(pr * PAIR, (pr + 1) * PAIR)
        kw = k_ref[pl.ds(start, nk), cols]
        vw = v_ref[pl.ds(start, nk), cols]
        q = q_ref[:, cols]
        out = None
        for c in range(2):
            qm = jnp.where(lo_q if c == 0 else ~lo_q, q, jnp.zeros_like(q))
            s = _dot_nt(qm, kw) + bias_ref[0, 2 * pr + c]
            m = jnp.max(s, axis=1, keepdims=True)
            e = jnp.exp(s - m)
            p = (e / jnp.sum(e, axis=1, keepdims=True)).astype(BF16)
            vm = jnp.where(lo_v if c == 0 else ~lo_v, vw, jnp.zeros_like(vw))
            part = _dot(p, vm)
            out = part if out is None else out + part
        o_ref[:, cols] = out.astype(o_ref.dtype)


def _na_bias_tables(rpb_i, grid_rows):
    nblk = grid_rows // NA_QROWS
    ndr, ndc = 2 * NA_WIN_ROWS - 1, 2 * NA_WIN_COLS - 1
    col = np.arange(GRID_W)
    col_start = np.clip(col - NA_WIN_COLS // 2, 0, GRID_W - NA_WIN_COLS)
    col_ok = (col[None, :] >= col_start[:, None]) & (col[None, :] < col_start[:, None] + NA_WIN_COLS)
    dc = np.clip(col[None, :] - col[:, None] + NA_WIN_COLS - 1, 0, ndc - 1)
    pick = (dc[None] == np.arange(ndc)[:, None, None]).astype(np.float32)
    blocks = jnp.einsum('hrd,dqk->hrqk', rpb_i.astype(F32), jnp.asarray(pick), precision=lax.Precision.HIGHEST)
    blocks = jnp.where(jnp.asarray(col_ok)[None, None], blocks, NEG)
    blocks = jnp.concatenate([blocks, jnp.full((NA_HEADS, 1, GRID_W, GRID_W), NEG, F32)], axis=1)
    tabs = []
    for blk in (0, 1, nblk - 1):
        start_row = int(np.clip(NA_QROWS * blk - NA_WIN_ROWS // 2, 0, grid_rows - NA_KROWS))
        qr = NA_QROWS * blk + np.arange(NA_QROWS)
        kr = start_row + np.arange(NA_KROWS)
        row_start = np.clip(qr - NA_WIN_ROWS // 2, 0, grid_rows - NA_WIN_ROWS)
        row_ok = (kr[None, :] >= row_start[:, None]) & (kr[None, :] < row_start[:, None] + NA_WIN_ROWS)
        dr = np.where(row_ok, kr[None, :] - qr[:, None] + NA_WIN_ROWS - 1, ndr)
        tab = jnp.concatenate(
            [jnp.concatenate([blocks[:, int(dr[a, b])] for b in range(NA_KROWS)], axis=-1)
             for a in range(NA_QROWS)], axis=-2)
        tabs.append(tab)
    return jnp.stack(tabs)


def _na_attention(proj, rpb_i, B, S):
    T = B * S
    grid_rows = S // GRID_W
    nblk = grid_rows // NA_QROWS
    nq, nk = NA_QROWS * GRID_W, NA_KROWS * GRID_W
    npairs = A_WIDTH // PAIR
    bias = _na_bias_tables(rpb_i, grid_rows)

    def cls(j):
        return jnp.where(j == 0, 0, jnp.where(j == nblk - 1, 2, 1))

    ppb = NA_PAIRS_PER_STEP
    ngroups = npairs // ppb
    wd = ppb * PAIR
    return pl.pallas_call(
        functools.partial(_na_kernel, grid_rows=grid_rows),
        grid=(B, ngroups, nblk),
        in_specs=[pl.BlockSpec((nq, wd), lambda b, p, j: (b * nblk + j, p)),
                  pl.BlockSpec((S, wd), lambda b, p, j: (b, ngroups + p)),
                  pl.BlockSpec((S, wd), lambda b, p, j: (b, 2 * ngroups + p)),
                  pl.BlockSpec((1, 2 * ppb, nq, nk), lambda b, p, j: (cls(j), p, 0, 0))],
        out_specs=pl.BlockSpec((nq, wd), lambda b, p, j: (b * nblk + j, p)),
        out_shape=jax.ShapeDtypeStruct((T, A_WIDTH), BF16),
        compiler_params=_params(("parallel", "parallel", "arbitrary")),
        name="na_attention",
    )(proj, proj, proj, bias)


def _diff_kernel(q_ref, k_ref, v_ref, lamp_ref, subg_ref, o_ref, m_ref, l_ref, acc_ref, *, lambda_init):
    kv = pl.program_id(3)

    @pl.when(kv == 0)
    def _():
        m_ref[...] = jnp.full(m_ref.shape, -jnp.inf, F32)
        l_ref[...] = jnp.zeros(l_ref.shape, F32)
        acc_ref[...] = jnp.zeros(acc_ref.shape, F32)

    q = q_ref[...]
    k = k_ref[...]
    v = v_ref[...]
    lo = _lo_lanes(q.shape)
    reps = k.shape[0] // LANES
    for c in range(2):
        qm = jnp.where(lo if c == 0 else ~lo, q, jnp.zeros_like(q))
        s = _dot_nt(qm, k)
        m_prev = m_ref[c]
        m_new = jnp.maximum(m_prev, jnp.max(s, axis=1, keepdims=True))
        alpha = jnp.exp2(m_prev - m_new)
        p = jnp.exp2(s - jnp.concatenate([m_new] * reps, axis=1))
        l_ref[c] = alpha * l_ref[c] + jnp.sum(p, axis=1, keepdims=True)
        acc_ref[c] = alpha * acc_ref[c] + _dot(p.astype(BF16), v)
        m_ref[c] = m_new

    @pl.when(kv == pl.num_programs(3) - 1)
    def _():
        lp = lamp_ref[...]
        lam = (jnp.exp(jnp.sum(lp[0:1] * lp[1:2], axis=1, keepdims=True))
               - jnp.exp(jnp.sum(lp[2:3] * lp[3:4], axis=1, keepdims=True)) + lambda_init)
        o = acc_ref[0] / l_ref[0] - lam * (acc_ref[1] / l_ref[1])
        o = o * lax.rsqrt(jnp.mean(o * o, axis=-1, keepdims=True) + LN_EPS) * subg_ref[...]
        o_ref[...] = (o * (1.0 - lambda_init)).astype(o_ref.dtype)


def _diff_attention(proj, lam_params, sub_g, lambda_init, B, S, tq=1024, tk=1024):
    T = B * S
    off = 3 * A_WIDTH // PAIR
    nh = DIFF_HEADS
    nq, nkv = S // tq, S // tk
    return pl.pallas_call(
        functools.partial(_diff_kernel, lambda_init=lambda_init),
        grid=(B, nh, nq, nkv),
        in_specs=[pl.BlockSpec((tq, PAIR), lambda b, h, i, j: (b * nq + i, off + h)),
                  pl.BlockSpec((tk, PAIR), lambda b, h, i, j: (b * nkv + j, off + nh + h)),
                  pl.BlockSpec((tk, PAIR), lambda b, h, i, j: (b * nkv + j, off + 2 * nh + h)),
                  pl.BlockSpec((4, HEAD_DIM), lambda b, h, i, j: (0, 0)),
                  pl.BlockSpec((1, PAIR), lambda b, h, i, j: (0, 0))],
        out_specs=pl.BlockSpec((tq, PAIR), lambda b, h, i, j: (b * nq + i, h)),
        out_shape=jax.ShapeDtypeStruct((T, B_WIDTH), BF16),
        scratch_shapes=[pltpu.VMEM((2, tq, LANES), F32),
                        pltpu.VMEM((2, tq, LANES), F32),
                        pltpu.VMEM((2, tq, PAIR), F32)],
        compiler_params=_params(("parallel", "parallel", "parallel", "arbitrary")),
        name="diff_attention",
    )(proj, proj, proj, lam_params, sub_g.reshape(1, PAIR))


def _swa_kernel(sink_ref, q_ref, kp_ref, km_ref, kn_ref, vp_ref, vm_ref, vn_ref, o_ref, kbuf, vbuf, *, nblocks):
    kvh = pl.program_id(1)
    qt = pl.program_id(2)
    blk = SWA_BLOCK
    nsub = q_ref.shape[0] // blk
    group = SWA_Q_HEADS // SWA_KV_HEADS
    kbuf[0:blk] = kp_ref[...]
    kbuf[blk:blk + nsub * blk] = km_ref[...]
    kbuf[blk + nsub * blk:] = kn_ref[...]
    vbuf[0:blk] = vp_ref[...]
    vbuf[blk:blk + nsub * blk] = vm_ref[...]
    vbuf[blk + nsub * blk:] = vn_ref[...]

    qi = lax.broadcasted_iota(jnp.int32, (blk, 3 * blk), 0)
    kk = lax.broadcasted_iota(jnp.int32, (blk, 3 * blk), 1)
    band = (kk - qi >= blk - SWA_WINDOW) & (kk - qi <= blk + SWA_WINDOW)
    lo = _lo_lanes((blk, PAIR))
    for j in range(nsub):
        n = qt * nsub + j
        first_key = jnp.where(n > 0, 0, blk)
        end_key = jnp.where(n < nblocks - 1, 3 * blk, 2 * blk)
        ok = band & (kk >= first_key) & (kk < end_key)
        kw = kbuf[j * blk:(j + 3) * blk]
        vw = vbuf[j * blk:(j + 3) * blk]
        qs = []
        for pr in range(group // 2):
            qp = q_ref[j * blk:(j + 1) * blk, pr * PAIR:(pr + 1) * PAIR]
            qs.append(jnp.where(lo, qp, jnp.zeros_like(qp)))
            qs.append(jnp.where(lo, jnp.zeros_like(qp), qp))
        s_all = _dot_nt(jnp.concatenate(qs, axis=0), kw)
        ps = []
        for g in range(group):
            sink = sink_ref[kvh * group + g]
            s = jnp.where(ok, s_all[g * blk:(g + 1) * blk], NEG)
            m = jnp.maximum(jnp.max(s, axis=1, keepdims=True), sink)
            e = jnp.exp(s - m)
            den = jnp.sum(e, axis=1, keepdims=True) + jnp.exp(sink - m)
            ps.append((e / den).astype(BF16))
        o_all = _dot(jnp.concatenate(ps, axis=0), vw)
        for pr in range(group // 2):
            o_pair = jnp.where(lo, o_all[(2 * pr) * blk:(2 * pr + 1) * blk],
                               o_all[(2 * pr + 1) * blk:(2 * pr + 2) * blk])
            o_ref[j * blk:(j + 1) * blk, pr * PAIR:(pr + 1) * PAIR] = o_pair.astype(o_ref.dtype)


def _swa_attention(proj, sink_j, B, S, tq=1024):
    T = B * S
    blk = SWA_BLOCK
    nsub = tq // blk
    nqt = S // tq
    nblocks = S // blk
    group = SWA_Q_HEADS // SWA_KV_HEADS
    qw = group * HEAD_DIM
    koff = C_Q // PAIR
    voff = koff + SWA_KV_HEADS

    def prev_map(off):
        return lambda b, h, i: (b * nblocks + jnp.maximum(i * nsub - 1, 0), off + h)

    def main_map(off):
        return lambda b, h, i: (b * nqt + i, off + h)

    def next_map(off):
        return lambda b, h, i: (b * nblocks + jnp.minimum(i * nsub + nsub, nblocks - 1), off + h)

    return pl.pallas_call(
        functools.partial(_swa_kernel, nblocks=nblocks),
        grid=(B, SWA_KV_HEADS, nqt),
        in_specs=[pl.BlockSpec(memory_space=pltpu.SMEM),
                  pl.BlockSpec((tq, qw), lambda b, h, i: (b * nqt + i, h)),
                  pl.BlockSpec((blk, PAIR), prev_map(koff)),
                  pl.BlockSpec((tq, PAIR), main_map(koff)),
                  pl.BlockSpec((blk, PAIR), next_map(koff)),
                  pl.BlockSpec((blk, PAIR), prev_map(voff)),
                  pl.BlockSpec((tq, PAIR), main_map(voff)),
                  pl.BlockSpec((blk, PAIR), next_map(voff))],
        out_specs=pl.BlockSpec((tq, qw), lambda b, h, i: (b * nqt + i, h)),
        out_shape=jax.ShapeDtypeStruct((T, C_Q), BF16),
        scratch_shapes=[pltpu.VMEM((tq + 2 * blk, PAIR), BF16),
                        pltpu.VMEM((tq + 2 * blk, PAIR), BF16)],
        compiler_params=_params(("parallel", "parallel", "parallel")),
        name="swa_attention",
    )(sink_j.astype(F32), proj, proj, proj, proj, proj, proj, proj)


def _extract_topk(s, k, want_rank=True):
    nrows = s.shape[0]
    row = lax.broadcasted_iota(jnp.int32, s.shape, 0).astype(F32)
    rank = jnp.full(s.shape, float(k), F32) if want_rank else None
    vals, firsts = [], []
    for r in range(k):
        m = jnp.max(s, axis=0, keepdims=True)
        first = jnp.min(jnp.where(s == m, row, float(nrows)), axis=0, keepdims=True)
        hit = row == first
        if want_rank:
            rank = jnp.where(hit, float(r), rank)
        s = jnp.where(hit, -jnp.inf, s)
        vals.append(m)
        firsts.append(first)
    return rank, jnp.concatenate(vals, axis=0), jnp.concatenate(firsts, axis=0)


def _dup_bf16_words(x):
    hi = pltpu.bitcast(x.astype(BF16).astype(F32), jnp.int32)
    return hi | lax.shift_right_logical(hi, 16)


def _peer_route_kernel(q_ref, keys_ref, n_ref, ea_ref, rb_ref, eb_ref):
    K = PEER_TOPK
    half = q_ref.shape[1] // 2
    q = q_ref[...]
    sa = _dot_nt(keys_ref[0, 0], q[:, :half])
    sb = _dot_nt(keys_ref[0, 1], q[:, half:])
    _, va, rows_a = _extract_topk(sa, K, want_rank=False)
    rank_b, vb, _ = _extract_topk(sb, K)

    ninf = jnp.full((8, sa.shape[1]), -jnp.inf, F32)
    sub = lax.broadcasted_iota(jnp.int32, ninf.shape, 0)
    pieces = [va[0:1] + vb]
    for ki in range(1, 8):
        pieces.append(jnp.where(sub < K // (ki + 1), va[ki:ki + 1] + vb[0:8], ninf))
    pieces.append(va[8:16] + vb[0:1])
    cand = jnp.concatenate(pieces, axis=0)
    rank_c, vc, _ = _extract_topk(cand, K)
    sel = (rank_c < K).astype(F32)
    counts = [jnp.sum(sel[0:16], axis=0, keepdims=True)]
    for ki in range(1, 8):
        counts.append(jnp.sum(sel[8 + 8 * ki:16 + 8 * ki], axis=0, keepdims=True))
    counts.append(sel[72:80])
    nvec = jnp.concatenate(counts, axis=0)
    z = jnp.sum(jnp.exp(vc - vc[0:1]), axis=0, keepdims=True)

    row = lax.broadcasted_iota(jnp.int32, sa.shape, 0).astype(F32)
    n_of_i = jnp.zeros(sa.shape, F32)
    for ki in range(K):
        n_of_i = jnp.where(row == rows_a[ki:ki + 1], nvec[ki:ki + 1], n_of_i)
    n_ref[0] = _dup_bf16_words(n_of_i)
    ea_ref[0] = _dup_bf16_words(jnp.exp(sa - va[0:1]) * (GELU_IN_SCALE / z))
    rb_ref[0] = pltpu.bitcast(rank_b.astype(BF16), jnp.int32)
    eb_ref[0] = pltpu.bitcast(jnp.exp(sb - vb[0:1]).astype(BF16), jnp.int32)


def _peer_route(q, sub_keys, tt=512):
    T = q.shape[0]
    H = PEER_HEADS
    kd = q.shape[1] // H
    shp = jax.ShapeDtypeStruct((H, PEER_N_KEYS, T), jnp.int32)
    shp16 = jax.ShapeDtypeStruct((H, PEER_N_KEYS // 2, T), jnp.int32)
    spec = pl.BlockSpec((1, PEER_N_KEYS, tt), lambda i, h: (h, 0, i))
    spec16 = pl.BlockSpec((1, PEER_N_KEYS // 2, tt), lambda i, h: (h, 0, i))
    return pl.pallas_call(
        _peer_route_kernel,
        grid=(T // tt, H),
        in_specs=[pl.BlockSpec((tt, kd), lambda i, h: (i, h)),
                  pl.BlockSpec((1, 2, PEER_N_KEYS, kd // 2), lambda i, h: (h, 0, 0, 0))],
        out_specs=[spec, spec, spec16, spec16],
        out_shape=[shp, shp, shp16, shp16],
        compiler_params=_params(("parallel", "parallel")),
        name="peer_route",
    )(q, sub_keys)


def _peer_dense_kernel(h_ref, u_ref, vt_ref, n_ref, ea_ref, rb_ref, eb_ref, x_ref, gate_ref, g_ref, b_ref,
                       out_ref, acc_ref, pre_ref, w_ref):
    et = pl.program_id(1)

    @pl.when(et == 0)
    def _():
        acc_ref[...] = jnp.zeros(acc_ref.shape, F32)

    nk = PEER_N_KEYS
    ni = u_ref.shape[0] // nk
    nlc = h_ref.shape[0] // LANES
    pack = 16
    assert ni == 8, "the per-i rows of one expert tile are read as one aligned 8-sublane block"
    i0 = pl.multiple_of(et * ni, ni)
    half = nk // 2
    igroup = 2

    def bcast_rows(blk, ii):
        row = pltpu.bitcast(jnp.broadcast_to(blk[ii:ii + 1], (pack // 2, LANES)), BF16)
        return jnp.concatenate([row] * (half // pack), axis=0)

    pre_ref[...] = _dot_nt(u_ref[...], h_ref[...])

    def gate_block(it, carry):
        lanes = pl.ds(pl.multiple_of((it // 2) * LANES, LANES), LANES)
        jh = it % 2
        jwords = pl.ds(pl.multiple_of(jh * (half // 2), half // 2), half // 2)
        for ig in range(0, ni, igroup):
            gates = [None] * igroup
            for h in range(PEER_HEADS):
                n_blk = n_ref[h, pl.ds(i0, ni), lanes]
                ea_blk = ea_ref[h, pl.ds(i0, ni), lanes]
                rb = pltpu.bitcast(rb_ref[h, jwords, lanes], BF16)
                eb = pltpu.bitcast(eb_ref[h, jwords, lanes], BF16)
                for k in range(igroup):
                    part = jnp.where(rb < bcast_rows(n_blk, ig + k), eb * bcast_rows(ea_blk, ig + k), 0)
                    gates[k] = part if gates[k] is None else gates[k] + part
            for k in range(igroup):
                rows = pl.ds(pl.multiple_of((ig + k) * nk + jh * half, half), half)
                pre = pre_ref[rows, lanes]
                act = pre * (1.0 + lax.erf(pre))
                w_ref[rows, lanes] = gates[k] * act.astype(BF16)
        return carry

    lax.fori_loop(0, 2 * nlc, gate_block, 0, unroll=8)
    acc_ref[...] += _dot(vt_ref[...], w_ref[...])

    @pl.when(et == pl.num_programs(1) - 1)
    def _():
        y = acc_ref[...].T
        out_ref[...] = _gated_ln(x_ref[...], y, gate_ref[0], g_ref[...], b_ref[...])


def _peer_dense(h, u, vt, routing, x, gate, g, b, S, tt=512, te=1024):
    T, D = x.shape
    E = u.shape[0]
    H, nk = PEER_HEADS, PEER_N_KEYS
    tiles_per_seq = S // tt
    rspec = pl.BlockSpec((H, nk, tt), lambda i, e: (0, 0, i))
    rspec16 = pl.BlockSpec((H, nk // 2, tt), lambda i, e: (0, 0, i))
    return pl.pallas_call(
        _peer_dense_kernel,
        grid=(T // tt, E // te),
        in_specs=[pl.BlockSpec((tt, D), lambda i, e: (i, 0)),
                  pl.BlockSpec((te, D), lambda i, e: (e, 0)),
                  pl.BlockSpec((D, te), lambda i, e: (0, e)),
                  rspec, rspec, rspec16, rspec16,
                  pl.BlockSpec((tt, D), lambda i, e: (i, 0)),
                  pl.BlockSpec((1, 1, D), lambda i, e: (i // tiles_per_seq, 0, 0)),
                  pl.BlockSpec((1, D), lambda i, e: (0, 0)),
                  pl.BlockSpec((1, D), lambda i, e: (0, 0))],
        out_specs=pl.BlockSpec((tt, D), lambda i, e: (i, 0)),
        out_shape=jax.ShapeDtypeStruct((T, D), F32),
        scratch_shapes=[pltpu.VMEM((D, tt), F32), pltpu.VMEM((te, tt), F32), pltpu.VMEM((te, tt), BF16)],
        compiler_params=_params(("parallel", "arbitrary")),
        name="peer_dense",
    )(h, u, vt, *routing, x, gate, g.reshape(1, D), b.reshape(1, D))


def _rope_tables(S):
    inv = 1.0 / (ROPE_THETA ** (jnp.arange(0, HEAD_DIM, 2, dtype=F32) / HEAD_DIM))
    ang = jnp.arange(S, dtype=F32)[:, None] * inv[None, :]
    cos, sin = jnp.cos(ang), jnp.sin(ang)
    reps = LANES // HEAD_DIM
    cs = jnp.concatenate([cos, cos] * reps, axis=1)
    sn = jnp.concatenate([-sin, sin] * reps, axis=1)
    return cs, sn


def _split3(m):
    B = m.shape[0]
    D = m.shape[1] // 3
    return (m[:, :D].reshape(B, 1, D), m[:, D:2 * D].reshape(B, 1, D), m[:, 2 * D:].reshape(B, 1, D))


def kernel(x, c, w_ada, b_ada, ln_g, ln_b, w_in_even, rpb, lam_q1, lam_k1, lam_q2, lam_k2, diff_sub_g,
           w_out_even, w_in_odd, sink, w_out_odd, peer_w_query, peer_sub_keys, peer_u, peer_v):
    B, S, D = x.shape
    T = B * S
    qscale = HEAD_DIM ** -0.5
    cs_tab, sn_tab = _rope_tables(S)
    mods = _ada_mods(c, w_ada, b_ada)
    xf = x.reshape(T, D).astype(F32)

    even_sections = ((0, A_WIDTH, False, qscale),
                     (A_WIDTH, 3 * A_WIDTH, False, 1.0),
                     (3 * A_WIDTH, 3 * A_WIDTH + B_WIDTH, True, qscale * math.log2(math.e)),
                     (3 * A_WIDTH + B_WIDTH, 3 * A_WIDTH + 2 * B_WIDTH, True, 1.0),
                     (3 * A_WIDTH + 2 * B_WIDTH, EVEN_IN, False, 1.0))
    dup = np.repeat(np.arange(SWA_KV_HEADS), 2)[:, None] * HEAD_DIM + np.arange(HEAD_DIM)[None, :]
    odd_cols = np.concatenate([np.arange(C_Q), C_Q + dup.reshape(-1), C_Q + C_KV + dup.reshape(-1)])
    odd_sections = ((0, C_Q, True, qscale),
                    (C_Q, C_Q + 2 * C_KV, True, 1.0),
                    (C_Q + 2 * C_KV, C_Q + 4 * C_KV, False, 1.0))
    peer_sections = ((0, peer_w_query.shape[-1], False, 1.0),)

    for layer in range(DEPTH):
        shift, scale, gate = _split3(mods[layer, 0])
        if layer % 2 == 0:
            i = layer // 2
            proj = _mod_proj(xf, shift, scale, w_in_even[i].astype(BF16), cs_tab, sn_tab, even_sections, S)
            o_a = _na_attention(proj, rpb[i], B, S)
            lambda_init = 0.8 - 0.6 * math.exp(-0.3 * layer)
            lam_params = jnp.stack([lam_q1[i], lam_k1[i], lam_q2[i], lam_k2[i]]).astype(F32)
            o_b = _diff_attention(proj, lam_params, diff_sub_g[i].astype(F32), lambda_init, B, S)
            xf = _proj_ln([o_a, o_b], w_out_even[i].astype(BF16), xf, gate, ln_g[layer, 0], ln_b[layer, 0], S)
        else:
            j = layer // 2
            proj = _mod_proj(xf, shift, scale, w_in_odd[j][:, odd_cols].astype(BF16), cs_tab, sn_tab,
                             odd_sections, S)
            o_c = _swa_attention(proj, sink[j], B, S)
            xf = _proj_ln([o_c], w_out_odd[j].astype(BF16), xf, gate, ln_g[layer, 0], ln_b[layer, 0], S)
        shift2, scale2, gate2 = _split3(mods[layer, 1])
        q, h2 = _mod_proj(xf, shift2, scale2, peer_w_query[layer].astype(BF16), cs_tab, sn_tab,
                          peer_sections, S, emit_h=True, h_mult=GELU_IN_SCALE)
        routing = _peer_route(q, peer_sub_keys[layer].astype(BF16))
        xf = _peer_dense(h2, peer_u[layer].astype(BF16), peer_v[layer].T.astype(BF16), routing,
                         xf, gate2, ln_g[layer, 1], ln_b[layer, 1], S)
    return xf.reshape(B, S, D).astype(x.dtype)
```

```python
import functools
import math

import numpy as np
import jax
import jax.numpy as jnp
from jax import lax
from jax.experimental import pallas as pl
from jax.experimental.pallas import tpu as pltpu

F32 = jnp.float32
BF16 = jnp.bfloat16

DEPTH = 4
GRID_W = 64
HEAD_DIM = 64
SWA_BLOCK = 128
NA_HEADS = 8
NA_WIN_ROWS = 8
NA_WIN_COLS = 16
DIFF_HEADS = 4
SWA_Q_HEADS = 16
SWA_KV_HEADS = 4
SWA_WINDOW = 128
PEER_HEADS = 8
PEER_N_KEYS = 128
PEER_TOPK = 16
ROPE_THETA = 10000.0
LN_EPS = 1e-5
NEG = -1e30
DEEPNORM_ALPHA = (2 * DEPTH) ** 0.25
GELU_IN_SCALE = math.sqrt(0.5)

A_WIDTH = NA_HEADS * HEAD_DIM
B_WIDTH = DIFF_HEADS * 2 * HEAD_DIM
EVEN_IN = 3 * A_WIDTH + 3 * B_WIDTH
C_Q = SWA_Q_HEADS * HEAD_DIM
C_KV = SWA_KV_HEADS * HEAD_DIM

LANES = 128
PAIR = 2 * HEAD_DIM
MXU_N = 256
VMEM_LIMIT = 48 * 1024 * 1024

NA_QROWS = 4
NA_KROWS = 12
NA_PAIRS_PER_STEP = 2


def _params(sem, vmem=VMEM_LIMIT):
    return pltpu.CompilerParams(dimension_semantics=sem, vmem_limit_bytes=vmem)


def _dot(a, b):
    return jnp.dot(a, b, preferred_element_type=F32)


def _dot_nt(a, b):
    return lax.dot_general(a, b, (((1,), (1,)), ((), ())), preferred_element_type=F32)


def _lo_lanes(shape):
    return (lax.broadcasted_iota(jnp.int32, shape, len(shape) - 1) % PAIR) < HEAD_DIM


def _ada_kernel(c_ref, w_ref, b_ref, o_ref):
    cond = jax.nn.silu(c_ref[...])
    o_ref[0] = jnp.dot(cond, w_ref[0], preferred_element_type=F32,
                       precision=lax.Precision.HIGHEST) + b_ref[0]


def _ada_mods(c, w_ada, b_ada):
    B, D = c.shape
    n = w_ada.shape[0] * w_ada.shape[1]
    E = w_ada.shape[-1]
    rows = 8
    cp = jnp.zeros((rows, D), F32).at[:B].set(c.astype(F32))
    tn = 1024
    out = pl.pallas_call(
        _ada_kernel,
        grid=(n, E // tn),
        in_specs=[pl.BlockSpec((rows, D), lambda i, j: (0, 0)),
                  pl.BlockSpec((1, D, tn), lambda i, j: (i, 0, j)),
                  pl.BlockSpec((1, 1, tn), lambda i, j: (i, 0, j))],
        out_specs=pl.BlockSpec((1, rows, tn), lambda i, j: (i, 0, j)),
        out_shape=jax.ShapeDtypeStruct((n, rows, E), F32),
        compiler_params=_params(("parallel", "parallel")),
        name="ada_mods",
    )(cp, w_ada.reshape(n, D, E), b_ada.reshape(n, 1, E))
    return out[:, :B].reshape(w_ada.shape[0], w_ada.shape[1], B, E)


def _mod_proj_kernel(x_ref, shift_ref, scale_ref, w_ref, cs_ref, sn_ref, o_ref, *h_out, sections, h_mult):
    hf = x_ref[...] * (1.0 + scale_ref[0]) + shift_ref[0]
    h = hf.astype(BF16)
    if h_out:
        h_out[0][...] = (hf * h_mult).astype(BF16)
    tm = h.shape[0]
    reps = MXU_N // LANES
    for (c0, c1, rope, mult) in sections:
        for cc in range(c0, c1, MXU_N):
            acc = _dot(h, w_ref[:, cc:cc + MXU_N])
            if rope:
                cs = jnp.concatenate([cs_ref[...]] * reps, axis=1)
                sn = jnp.concatenate([sn_ref[...]] * reps, axis=1)
                first_half = (lax.broadcasted_iota(jnp.int32, (tm, MXU_N), 1) % HEAD_DIM) < HEAD_DIM // 2
                partner = jnp.where(first_half,
                                    pltpu.roll(acc, MXU_N - HEAD_DIM // 2, 1),
                                    pltpu.roll(acc, HEAD_DIM // 2, 1))
                acc = acc * cs + partner * sn
            if mult != 1.0:
                acc = acc * mult
            o_ref[:, cc:cc + MXU_N] = acc.astype(o_ref.dtype)


def _mod_proj(x, shift, scale, w, cs_tab, sn_tab, sections, S, emit_h=False, h_mult=1.0, tm=512):
    T, D = x.shape
    N = w.shape[1]
    tiles_per_seq = S // tm
    out_shape = [jax.ShapeDtypeStruct((T, N), BF16)]
    out_specs = [pl.BlockSpec((tm, N), lambda i: (i, 0))]
    if emit_h:
        out_shape.append(jax.ShapeDtypeStruct((T, D), BF16))
        out_specs.append(pl.BlockSpec((tm, D), lambda i: (i, 0)))
    res = pl.pallas_call(
        functools.partial(_mod_proj_kernel, sections=sections, h_mult=h_mult),
        grid=(T // tm,),
        in_specs=[pl.BlockSpec((tm, D), lambda i: (i, 0)),
                  pl.BlockSpec((1, 1, D), lambda i: (i // tiles_per_seq, 0, 0)),
                  pl.BlockSpec((1, 1, D), lambda i: (i // tiles_per_seq, 0, 0)),
                  pl.BlockSpec((D, N), lambda i: (0, 0)),
                  pl.BlockSpec((tm, LANES), lambda i: (i % tiles_per_seq, 0)),
                  pl.BlockSpec((tm, LANES), lambda i: (i % tiles_per_seq, 0))],
        out_specs=out_specs,
        out_shape=out_shape,
        compiler_params=_params(("parallel",)),
        name="mod_proj",
    )(x, shift, scale, w, cs_tab, sn_tab)
    return res if emit_h else res[0]


def _gated_ln(x, y, gate, g, b):
    z = DEEPNORM_ALPHA * x + gate * y
    mu = jnp.mean(z, axis=-1, keepdims=True)
    zc = z - mu
    var = jnp.mean(zc * zc, axis=-1, keepdims=True)
    return zc * lax.rsqrt(var + LN_EPS) * g + b


def _proj_ln_kernel(*refs, widths):
    n = len(widths)
    o_refs = refs[:n]
    w_ref, x_ref, gate_ref, g_ref, b_ref, out_ref = refs[n:]
    y = None
    k0 = 0
    for o_ref, wd in zip(o_refs, widths):
        part = _dot(o_ref[...], w_ref[k0:k0 + wd, :])
        y = part if y is None else y + part
        k0 += wd
    out_ref[...] = _gated_ln(x_ref[...], y, gate_ref[0], g_ref[...], b_ref[...])


def _proj_ln(o_list, w, x, gate, g, b, S, tm=512):
    T, D = x.shape
    widths = tuple(o.shape[1] for o in o_list)
    tiles_per_seq = S // tm
    in_specs = [pl.BlockSpec((tm, wd), lambda i: (i, 0)) for wd in widths]
    in_specs += [pl.BlockSpec((sum(widths), D), lambda i: (0, 0)),
                 pl.BlockSpec((tm, D), lambda i: (i, 0)),
                 pl.BlockSpec((1, 1, D), lambda i: (i // tiles_per_seq, 0, 0)),
                 pl.BlockSpec((1, D), lambda i: (0, 0)),
                 pl.BlockSpec((1, D), lambda i: (0, 0))]
    return pl.pallas_call(
        functools.partial(_proj_ln_kernel, widths=widths),
        grid=(T // tm,),
        in_specs=in_specs,
        out_specs=pl.BlockSpec((tm, D), lambda i: (i, 0)),
        out_shape=jax.ShapeDtypeStruct((T, D), F32),
        compiler_params=_params(("parallel",)),
        name="proj_ln",
    )(*o_list, w, x, gate, g.reshape(1, D), b.reshape(1, D))


def _na_kernel(q_ref, k_ref, v_ref, bias_ref, o_ref, *, grid_rows):
    blk = pl.program_id(2)
    start_row = jnp.clip(NA_QROWS * blk - NA_WIN_ROWS // 2, 0, grid_rows - NA_KROWS)
    start = pl.multiple_of(start_row * GRID_W, GRID_W)
    nk = NA_KROWS * GRID_W
    lo_q = _lo_lanes((q_ref.shape[0], PAIR))
    lo_v = _lo_lanes((nk, PAIR))
    for pr in range(q_ref.shape[1] // PAIR):
        cols = slice(pr * PAIR, (pr + 1) * PAIR)
        kw = k_ref[pl.ds(start, nk), cols]
        vw = v_ref[pl.ds(start, nk), cols]
        q = q_ref[:, cols]
        out = None
        for c in range(2):
            qm = jnp.where(lo_q if c == 0 else ~lo_q, q, jnp.zeros_like(q))
            s = _dot_nt(qm, kw) + bias_ref[0, 2 * pr + c]
            m = jnp.max(s, axis=1, keepdims=True)
            e = jnp.exp(s - m)
            p = (e / jnp.sum(e, axis=1, keepdims=True)).astype(BF16)
            vm = jnp.where(lo_v if c == 0 else ~lo_v, vw, jnp.zeros_like(vw))
            part = _dot(p, vm)
            out = part if out is None else out + part
        o_ref[:, cols] = out.astype(o_ref.dtype)


def _na_bias_tables(rpb_i, grid_rows):
    nblk = grid_rows // NA_QROWS
    ndr, ndc = 2 * NA_WIN_ROWS - 1, 2 * NA_WIN_COLS - 1
    col = np.arange(GRID_W)
    col_start = np.clip(col - NA_WIN_COLS // 2, 0, GRID_W - NA_WIN_COLS)
    col_ok = (col[None, :] >= col_start[:, None]) & (col[None, :] < col_start[:, None] + NA_WIN_COLS)
    dc = np.clip(col[None, :] - col[:, None] + NA_WIN_COLS - 1, 0, ndc - 1)
    pick = (dc[None] == np.arange(ndc)[:, None, None]).astype(np.float32)
    blocks = jnp.einsum('hrd,dqk->hrqk', rpb_i.astype(F32), jnp.asarray(pick), precision=lax.Precision.HIGHEST)
    blocks = jnp.where(jnp.asarray(col_ok)[None, None], blocks, NEG)
    blocks = jnp.concatenate([blocks, jnp.full((NA_HEADS, 1, GRID_W, GRID_W), NEG, F32)], axis=1)
    tabs = []
    for blk in (0, 1, nblk - 1):
        start_row = int(np.clip(NA_QROWS * blk - NA_WIN_ROWS // 2, 0, grid_rows - NA_KROWS))
        qr = NA_QROWS * blk + np.arange(NA_QROWS)
        kr = start_row + np.arange(NA_KROWS)
        row_start = np.clip(qr - NA_WIN_ROWS // 2, 0, grid_rows - NA_WIN_ROWS)
        row_ok = (kr[None, :] >= row_start[:, None]) & (kr[None, :] < row_start[:, None] + NA_WIN_ROWS)
        dr = np.where(row_ok, kr[None, :] - qr[:, None] + NA_WIN_ROWS - 1, ndr)
        tab = jnp.concatenate(
            [jnp.concatenate([blocks[:, int(dr[a, b])] for b in range(NA_KROWS)], axis=-1)
             for a in range(NA_QROWS)], axis=-2)
        tabs.append(tab)
    return jnp.stack(tabs)


def _na_attention(proj, rpb_i, B, S):
    T = B * S
    grid_rows = S // GRID_W
    nblk = grid_rows // NA_QROWS
    nq, nk = NA_QROWS * GRID_W, NA_KROWS * GRID_W
    npairs = A_WIDTH // PAIR
    bias = _na_bias_tables(rpb_i, grid_rows)

    def cls(j):
        return jnp.where(j == 0, 0, jnp.where(j == nblk - 1, 2, 1))

    ppb = NA_PAIRS_PER_STEP
    ngroups = npairs // ppb
    wd = ppb * PAIR
    return pl.pallas_call(
        functools.partial(_na_kernel, grid_rows=grid_rows),
        grid=(B, ngroups, nblk),
        in_specs=[pl.BlockSpec((nq, wd), lambda b, p, j: (b * nblk + j, p)),
                  pl.BlockSpec((S, wd), lambda b, p, j: (b, ngroups + p)),
                  pl.BlockSpec((S, wd), lambda b, p, j: (b, 2 * ngroups + p)),
                  pl.BlockSpec((1, 2 * ppb, nq, nk), lambda b, p, j: (cls(j), p, 0, 0))],
        out_specs=pl.BlockSpec((nq, wd), lambda b, p, j: (b * nblk + j, p)),
        out_shape=jax.ShapeDtypeStruct((T, A_WIDTH), BF16),
        compiler_params=_params(("parallel", "parallel", "arbitrary")),
        name="na_attention",
    )(proj, proj, proj, bias)


def _diff_kernel(q_ref, k_ref, v_ref, lamp_ref, subg_ref, o_ref, m_ref, l_ref, acc_ref, *, lambda_init):
    kv = pl.program_id(3)

    @pl.when(kv == 0)
    def _():
        m_ref[...] = jnp.full(m_ref.shape, -jnp.inf, F32)
        l_ref[...] = jnp.zeros(l_ref.shape, F32)
        acc_ref[...] = jnp.zeros(acc_ref.shape, F32)

    q = q_ref[...]
    k = k_ref[...]
    v = v_ref[...]
    lo = _lo_lanes(q.shape)
    reps = k.shape[0] // LANES
    for c in range(2):
        qm = jnp.where(lo if c == 0 else ~lo, q, jnp.zeros_like(q))
        s = _dot_nt(qm, k)
        m_prev = m_ref[c]
        m_new = jnp.maximum(m_prev, jnp.max(s, axis=1, keepdims=True))
        alpha = jnp.exp2(m_prev - m_new)
        p = jnp.exp2(s - jnp.concatenate([m_new] * reps, axis=1))
        l_ref[c] = alpha * l_ref[c] + jnp.sum(p, axis=1, keepdims=True)
        acc_ref[c] = alpha * acc_ref[c] + _dot(p.astype(BF16), v)
        m_ref[c] = m_new

    @pl.when(kv == pl.num_programs(3) - 1)
    def _():
        lp = lamp_ref[...]
        lam = (jnp.exp(jnp.sum(lp[0:1] * lp[1:2], axis=1, keepdims=True))
               - jnp.exp(jnp.sum(lp[2:3] * lp[3:4], axis=1, keepdims=True)) + lambda_init)
        o = acc_ref[0] / l_ref[0] - lam * (acc_ref[1] / l_ref[1])
        o = o * lax.rsqrt(jnp.mean(o * o, axis=-1, keepdims=True) + LN_EPS) * subg_ref[...]
        o_ref[...] = (o * (1.0 - lambda_init)).astype(o_ref.dtype)


def _diff_attention(proj, lam_params, sub_g, lambda_init, B, S, tq=1024, tk=2048):
    T = B * S
    off = 3 * A_WIDTH // PAIR
    nh = DIFF_HEADS
    nq, nkv = S // tq, S // tk
    return pl.pallas_call(
        functools.partial(_diff_kernel, lambda_init=lambda_init),
        grid=(B, nh, nq, nkv),
        in_specs=[pl.BlockSpec((tq, PAIR), lambda b, h, i, j: (b * nq + i, off + h)),
                  pl.BlockSpec((tk, PAIR), lambda b, h, i, j: (b * nkv + j, off + nh + h)),
                  pl.BlockSpec((tk, PAIR), lambda b, h, i, j: (b * nkv + j, off + 2 * nh + h)),
                  pl.BlockSpec((4, HEAD_DIM), lambda b, h, i, j: (0, 0)),
                  pl.BlockSpec((1, PAIR), lambda b, h, i, j: (0, 0))],
        out_specs=pl.BlockSpec((tq, PAIR), lambda b, h, i, j: (b * nq + i, h)),
        out_shape=jax.ShapeDtypeStruct((T, B_WIDTH), BF16),
        scratch_shapes=[pltpu.VMEM((2, tq, LANES), F32),
                        pltpu.VMEM((2, tq, LANES), F32),
                        pltpu.VMEM((2, tq, PAIR), F32)],
        compiler_params=_params(("parallel", "parallel", "parallel", "arbitrary")),
        name="diff_attention",
    )(proj, proj, proj, lam_params, sub_g.reshape(1, PAIR))


def _swa_kernel(sink_ref, q_ref, kp_ref, km_ref, kn_ref, vp_ref, vm_ref, vn_ref, o_ref, kbuf, vbuf, *, nblocks):
    kvh = pl.program_id(1)
    qt = pl.program_id(2)
    blk = SWA_BLOCK
    nsub = q_ref.shape[0] // blk
    group = SWA_Q_HEADS // SWA_KV_HEADS
    kbuf[0:blk] = kp_ref[...]
    kbuf[blk:blk + nsub * blk] = km_ref[...]
    kbuf[blk + nsub * blk:] = kn_ref[...]
    vbuf[0:blk] = vp_ref[...]
    vbuf[blk:blk + nsub * blk] = vm_ref[...]
    vbuf[blk + nsub * blk:] = vn_ref[...]

    qi = lax.broadcasted_iota(jnp.int32, (blk, 3 * blk), 0)
    kk = lax.broadcasted_iota(jnp.int32, (blk, 3 * blk), 1)
    band = (kk - qi >= blk - SWA_WINDOW) & (kk - qi <= blk + SWA_WINDOW)
    lo = _lo_lanes((blk, PAIR))
    for j in range(nsub):
        n = qt * nsub + j
        first_key = jnp.where(n > 0, 0, blk)
        end_key = jnp.where(n < nblocks - 1, 3 * blk, 2 * blk)
        ok = band & (kk >= first_key) & (kk < end_key)
        kw = kbuf[j * blk:(j + 3) * blk]
        vw = vbuf[j * blk:(j + 3) * blk]
        qs = []
        for pr in range(group // 2):
            qp = q_ref[j * blk:(j + 1) * blk, pr * PAIR:(pr + 1) * PAIR]
            qs.append(jnp.where(lo, qp, jnp.zeros_like(qp)))
            qs.append(jnp.where(lo, jnp.zeros_like(qp), qp))
        s_all = _dot_nt(jnp.concatenate(qs, axis=0), kw)
        ps = []
        for g in range(group):
            sink = sink_ref[kvh * group + g]
            s = jnp.where(ok, s_all[g * blk:(g + 1) * blk], NEG)
            m = jnp.maximum(jnp.max(s, axis=1, keepdims=True), sink)
            e = jnp.exp(s - m)
            den = jnp.sum(e, axis=1, keepdims=True) + jnp.exp(sink - m)
            ps.append((e / den).astype(BF16))
        o_all = _dot(jnp.concatenate(ps, axis=0), vw)
        for pr in range(group // 2):
            o_pair = jnp.where(lo, o_all[(2 * pr) * blk:(2 * pr + 1) * blk],
                               o_all[(2 * pr + 1) * blk:(2 * pr + 2) * blk])
            o_ref[j * blk:(j + 1) * blk, pr * PAIR:(pr + 1) * PAIR] = o_pair.astype(o_ref.dtype)


def _swa_attention(proj, sink_j, B, S, tq=1024):
    T = B * S
    blk = SWA_BLOCK
    nsub = tq // blk
    nqt = S // tq
    nblocks = S // blk
    group = SWA_Q_HEADS // SWA_KV_HEADS
    qw = group * HEAD_DIM
    koff = C_Q // PAIR
    voff = koff + SWA_KV_HEADS

    def prev_map(off):
        return lambda b, h, i: (b * nblocks + jnp.maximum(i * nsub - 1, 0), off + h)

    def main_map(off):
        return lambda b, h, i: (b * nqt + i, off + h)

    def next_map(off):
        return lambda b, h, i: (b * nblocks + jnp.minimum(i * nsub + nsub, nblocks - 1), off + h)

    return pl.pallas_call(
        functools.partial(_swa_kernel, nblocks=nblocks),
        grid=(B, SWA_KV_HEADS, nqt),
        in_specs=[pl.BlockSpec(memory_space=pltpu.SMEM),
                  pl.BlockSpec((tq, qw), lambda b, h, i: (b * nqt + i, h)),
                  pl.BlockSpec((blk, PAIR), prev_map(koff)),
                  pl.BlockSpec((tq, PAIR), main_map(koff)),
                  pl.BlockSpec((blk, PAIR), next_map(koff)),
                  pl.BlockSpec((blk, PAIR), prev_map(voff)),
                  pl.BlockSpec((tq, PAIR), main_map(voff)),
                  pl.BlockSpec((blk, PAIR), next_map(voff))],
        out_specs=pl.BlockSpec((tq, qw), lambda b, h, i: (b * nqt + i, h)),
        out_shape=jax.ShapeDtypeStruct((T, C_Q), BF16),
        scratch_shapes=[pltpu.VMEM((tq + 2 * blk, PAIR), BF16),
                        pltpu.VMEM((tq + 2 * blk, PAIR), BF16)],
        compiler_params=_params(("parallel", "parallel", "parallel")),
        name="swa_attention",
    )(sink_j.astype(F32), proj, proj, proj, proj, proj, proj, proj)


def _extract_topk(s, k, want_rank=True):
    nrows = s.shape[0]
    row = lax.broadcasted_iota(jnp.int32, s.shape, 0).astype(F32)
    rank = jnp.full(s.shape, float(k), F32) if want_rank else None
    vals, firsts = [], []
    for r in range(k):
        m = jnp.max(s, axis=0, keepdims=True)
        first = jnp.min(jnp.where(s == m, row, float(nrows)), axis=0, keepdims=True)
        hit = row == first
        if want_rank:
            rank = jnp.where(hit, float(r), rank)
        s = jnp.where(hit, -jnp.inf, s)
        vals.append(m)
        firsts.append(first)
    return rank, jnp.concatenate(vals, axis=0), jnp.concatenate(firsts, axis=0)


def _dup_bf16_words(x):
    hi = pltpu.bitcast(x.astype(BF16).astype(F32), jnp.int32)
    return hi | lax.shift_right_logical(hi, 16)


def _peer_route_kernel(q_ref, keys_ref, n_ref, ea_ref, rb_ref, eb_ref):
    K = PEER_TOPK
    half = q_ref.shape[1] // 2
    q = q_ref[...]
    sa = _dot_nt(keys_ref[0, 0], q[:, :half])
    sb = _dot_nt(keys_ref[0, 1], q[:, half:])
    _, va, rows_a = _extract_topk(sa, K, want_rank=False)
    rank_b, vb, _ = _extract_topk(sb, K)

    ninf = jnp.full((8, sa.shape[1]), -jnp.inf, F32)
    sub = lax.broadcasted_iota(jnp.int32, ninf.shape, 0)
    pieces = [va[0:1] + vb]
    for ki in range(1, 8):
        pieces.append(jnp.where(sub < K // (ki + 1), va[ki:ki + 1] + vb[0:8], ninf))
    pieces.append(va[8:16] + vb[0:1])
    cand = jnp.concatenate(pieces, axis=0)
    rank_c, vc, _ = _extract_topk(cand, K)
    sel = (rank_c < K).astype(F32)
    counts = [jnp.sum(sel[0:16], axis=0, keepdims=True)]
    for ki in range(1, 8):
        counts.append(jnp.sum(sel[8 + 8 * ki:16 + 8 * ki], axis=0, keepdims=True))
    counts.append(sel[72:80])
    nvec = jnp.concatenate(counts, axis=0)
    z = jnp.sum(jnp.exp(vc - vc[0:1]), axis=0, keepdims=True)

    row = lax.broadcasted_iota(jnp.int32, sa.shape, 0).astype(F32)
    n_of_i = jnp.zeros(sa.shape, F32)
    for ki in range(K):
        n_of_i = jnp.where(row == rows_a[ki:ki + 1], nvec[ki:ki + 1], n_of_i)
    n_ref[0] = _dup_bf16_words(n_of_i)
    ea_ref[0] = _dup_bf16_words(jnp.exp(sa - va[0:1]) * (GELU_IN_SCALE / z))
    rb_ref[0] = pltpu.bitcast(rank_b.astype(BF16), jnp.int32)
    eb_ref[0] = pltpu.bitcast(jnp.exp(sb - vb[0:1]).astype(BF16), jnp.int32)


def _peer_route(q, sub_keys, tt=1024):
    T = q.shape[0]
    H = PEER_HEADS
    kd = q.shape[1] // H
    shp = jax.ShapeDtypeStruct((H, PEER_N_KEYS, T), jnp.int32)
    shp16 = jax.ShapeDtypeStruct((H, PEER_N_KEYS // 2, T), jnp.int32)
    spec = pl.BlockSpec((1, PEER_N_KEYS, tt), lambda i, h: (h, 0, i))
    spec16 = pl.BlockSpec((1, PEER_N_KEYS // 2, tt), lambda i, h: (h, 0, i))
    return pl.pallas_call(
        _peer_route_kernel,
        grid=(T // tt, H),
        in_specs=[pl.BlockSpec((tt, kd), lambda i, h: (i, h)),
                  pl.BlockSpec((1, 2, PEER_N_KEYS, kd // 2), lambda i, h: (h, 0, 0, 0))],
        out_specs=[spec, spec, spec16, spec16],
        out_shape=[shp, shp, shp16, shp16],
        compiler_params=_params(("parallel", "parallel")),
        name="peer_route",
    )(q, sub_keys)


def _peer_dense_kernel(h_ref, u_ref, vt_ref, n_ref, ea_ref, rb_ref, eb_ref, x_ref, gate_ref, g_ref, b_ref,
                       out_ref, acc_ref, pre_ref, w_ref):
    et = pl.program_id(1)

    @pl.when(et == 0)
    def _():
        acc_ref[...] = jnp.zeros(acc_ref.shape, F32)

    nk = PEER_N_KEYS
    ni = u_ref.shape[0] // nk
    nlc = h_ref.shape[0] // LANES
    pack = 16
    assert ni == 8, "the per-i rows of one expert tile are read as one aligned 8-sublane block"
    i0 = pl.multiple_of(et * ni, ni)
    half = nk // 2
    igroup = 2

    def bcast_rows(blk, ii):
        row = pltpu.bitcast(jnp.broadcast_to(blk[ii:ii + 1], (pack // 2, LANES)), BF16)
        return jnp.concatenate([row] * (half // pack), axis=0)

    pre_ref[...] = _dot_nt(u_ref[...], h_ref[...])

    def gate_block(it, carry):
        lanes = pl.ds(pl.multiple_of((it // 2) * LANES, LANES), LANES)
        jh = it % 2
        jwords = pl.ds(pl.multiple_of(jh * (half // 2), half // 2), half // 2)
        for ig in range(0, ni, igroup):
            gates = [None] * igroup
            for h in range(PEER_HEADS):
                n_blk = n_ref[h, pl.ds(i0, ni), lanes]
                ea_blk = ea_ref[h, pl.ds(i0, ni), lanes]
                rb = pltpu.bitcast(rb_ref[h, jwords, lanes], BF16)
                eb = pltpu.bitcast(eb_ref[h, jwords, lanes], BF16)
                for k in range(igroup):
                    part = jnp.where(rb < bcast_rows(n_blk, ig + k), eb * bcast_rows(ea_blk, ig + k), 0)
                    gates[k] = part if gates[k] is None else gates[k] + part
            for k in range(igroup):
                rows = pl.ds(pl.multiple_of((ig + k) * nk + jh * half, half), half)
                pre = pre_ref[rows, lanes]
                act = pre * (1.0 + lax.erf(pre))
                w_ref[rows, lanes] = gates[k] * act.astype(BF16)
        return carry

    lax.fori_loop(0, 2 * nlc, gate_block, 0, unroll=8)
    acc_ref[...] += _dot(vt_ref[...], w_ref[...])

    @pl.when(et == pl.num_programs(1) - 1)
    def _():
        y = acc_ref[...].T
        out_ref[...] = _gated_ln(x_ref[...], y, gate_ref[0], g_ref[...], b_ref[...])


def _peer_dense(h, u, vt, routing, x, gate, g, b, S, tt=512, te=1024):
    T, D = x.shape
    E = u.shape[0]
    H, nk = PEER_HEADS, PEER_N_KEYS
    tiles_per_seq = S // tt
    rspec = pl.BlockSpec((H, nk, tt), lambda i, e: (0, 0, i))
    rspec16 = pl.BlockSpec((H, nk // 2, tt), lambda i, e: (0, 0, i))
    return pl.pallas_call(
        _peer_dense_kernel,
        grid=(T // tt, E // te),
        in_specs=[pl.BlockSpec((tt, D), lambda i, e: (i, 0)),
                  pl.BlockSpec((te, D), lambda i, e: (e, 0)),
                  pl.BlockSpec((D, te), lambda i, e: (0, e)),
                  rspec, rspec, rspec16, rspec16,
                  pl.BlockSpec((tt, D), lambda i, e: (i, 0)),
                  pl.BlockSpec((1, 1, D), lambda i, e: (i // tiles_per_seq, 0, 0)),
                  pl.BlockSpec((1, D), lambda i, e: (0, 0)),
                  pl.BlockSpec((1, D), lambda i, e: (0, 0))],
        out_specs=pl.BlockSpec((tt, D), lambda i, e: (i, 0)),
        out_shape=jax.ShapeDtypeStruct((T, D), F32),
        scratch_shapes=[pltpu.VMEM((D, tt), F32), pltpu.VMEM((te, tt), F32), pltpu.VMEM((te, tt), BF16)],
        compiler_params=_params(("parallel", "arbitrary")),
        name="peer_dense",
    )(h, u, vt, *routing, x, gate, g.reshape(1, D), b.reshape(1, D))


def _rope_tables(S):
    inv = 1.0 / (ROPE_THETA ** (jnp.arange(0, HEAD_DIM, 2, dtype=F32) / HEAD_DIM))
    ang = jnp.arange(S, dtype=F32)[:, None] * inv[None, :]
    cos, sin = jnp.cos(ang), jnp.sin(ang)
    reps = LANES // HEAD_DIM
    cs = jnp.concatenate([cos, cos] * reps, axis=1)
    sn = jnp.concatenate([-sin, sin] * reps, axis=1)
    return cs, sn


def _split3(m):
    B = m.shape[0]
    D = m.shape[1] // 3
    return (m[:, :D].reshape(B, 1, D), m[:, D:2 * D].reshape(B, 1, D), m[:, 2 * D:].reshape(B, 1, D))


def kernel(x, c, w_ada, b_ada, ln_g, ln_b, w_in_even, rpb, lam_q1, lam_k1, lam_q2, lam_k2, diff_sub_g,
           w_out_even, w_in_odd, sink, w_out_odd, peer_w_query, peer_sub_keys, peer_u, peer_v):
    B, S, D = x.shape
    T = B * S
    qscale = HEAD_DIM ** -0.5
    cs_tab, sn_tab = _rope_tables(S)
    mods = _ada_mods(c, w_ada, b_ada)
    xf = x.reshape(T, D).astype(F32)

    even_sections = ((0, A_WIDTH, False, qscale),
                     (A_WIDTH, 3 * A_WIDTH, False, 1.0),
                     (3 * A_WIDTH, 3 * A_WIDTH + B_WIDTH, True, qscale * math.log2(math.e)),
                     (3 * A_WIDTH + B_WIDTH, 3 * A_WIDTH + 2 * B_WIDTH, True, 1.0),
                     (3 * A_WIDTH + 2 * B_WIDTH, EVEN_IN, False, 1.0))
    dup = np.repeat(np.arange(SWA_KV_HEADS), 2)[:, None] * HEAD_DIM + np.arange(HEAD_DIM)[None, :]
    odd_cols = np.concatenate([np.arange(C_Q), C_Q + dup.reshape(-1), C_Q + C_KV + dup.reshape(-1)])
    odd_sections = ((0, C_Q, True, qscale),
                    (C_Q, C_Q + 2 * C_KV, True, 1.0),
                    (C_Q + 2 * C_KV, C_Q + 4 * C_KV, False, 1.0))
    peer_sections = ((0, peer_w_query.shape[-1], False, 1.0),)

    for layer in range(DEPTH):
        shift, scale, gate = _split3(mods[layer, 0])
        if layer % 2 == 0:
            i = layer // 2
            proj = _mod_proj(xf, shift, scale, w_in_even[i].astype(BF16), cs_tab, sn_tab, even_sections, S)
            o_a = _na_attention(proj, rpb[i], B, S)
            lambda_init = 0.8 - 0.6 * math.exp(-0.3 * layer)
            lam_params = jnp.stack([lam_q1[i], lam_k1[i], lam_q2[i], lam_k2[i]]).astype(F32)
            o_b = _diff_attention(proj, lam_params, diff_sub_g[i].astype(F32), lambda_init, B, S)
            xf = _proj_ln([o_a, o_b], w_out_even[i].astype(BF16), xf, gate, ln_g[layer, 0], ln_b[layer, 0], S)
        else:
            j = layer // 2
            proj = _mod_proj(xf, shift, scale, w_in_odd[j][:, odd_cols].astype(BF16), cs_tab, sn_tab,
                             odd_sections, S)
            o_c = _swa_attention(proj, sink[j], B, S)
            xf = _proj_ln([o_c], w_out_odd[j].astype(BF16), xf, gate, ln_g[layer, 0], ln_b[layer, 0], S)
        shift2, scale2, gate2 = _split3(mods[layer, 1])
        q, h2 = _mod_proj(xf, shift2, scale2, peer_w_query[layer].astype(BF16), cs_tab, sn_tab,
                          peer_sections, S, emit_h=True, h_mult=GELU_IN_SCALE)
        routing = _peer_route(q, peer_sub_keys[layer].astype(BF16))
        xf = _peer_dense(h2, peer_u[layer].astype(BF16), peer_v[layer].T.astype(BF16), routing,
                         xf, gate2, ln_g[layer, 1], ln_b[layer, 1], S)
    return xf.reshape(B, S, D).astype(x.dtype)
```

```python
import functools
import math

import numpy as np
import jax
import jax.numpy as jnp
from jax import lax
from jax.experimental import pallas as pl
from jax.experimental.pallas import tpu as pltpu

F32 = jnp.float32
BF16 = jnp.bfloat16

DEPTH = 4
GRID_W = 64
HEAD_DIM = 64
SWA_BLOCK = 128
NA_HEADS = 8
NA_WIN_ROWS = 8
NA_WIN_COLS = 16
DIFF_HEADS = 4
SWA_Q_HEADS = 16
SWA_KV_HEADS = 4
SWA_WINDOW = 128
PEER_HEADS = 8
PEER_N_KEYS = 128
PEER_TOPK = 16
ROPE_THETA = 10000.0
LN_EPS = 1e-5
NEG = -1e30
DEEPNORM_ALPHA = (2 * DEPTH) ** 0.25
GELU_IN_SCALE = math.sqrt(0.5)

A_WIDTH = NA_HEADS * HEAD_DIM
B_WIDTH = DIFF_HEADS * 2 * HEAD_DIM
EVEN_IN = 3 * A_WIDTH + 3 * B_WIDTH
C_Q = SWA_Q_HEADS * HEAD_DIM
C_KV = SWA_KV_HEADS * HEAD_DIM

LANES = 128
PAIR = 2 * HEAD_DIM
MXU_N = 256
VMEM_LIMIT = 48 * 1024 * 1024

NA_QROWS = 4
NA_KROWS = 12
NA_PAIRS_PER_STEP = 2


def _params(sem, vmem=VMEM_LIMIT):
    return pltpu.CompilerParams(dimension_semantics=sem, vmem_limit_bytes=vmem)


def _dot(a, b):
    return jnp.dot(a, b, preferred_element_type=F32)


def _dot_nt(a, b):
    return lax.dot_general(a, b, (((1,), (1,)), ((), ())), preferred_element_type=F32)


def _lo_lanes(shape):
    return (lax.broadcasted_iota(jnp.int32, shape, len(shape) - 1) % PAIR) < HEAD_DIM


def _ada_kernel(c_ref, w_ref, b_ref, o_ref):
    cond = jax.nn.silu(c_ref[...])
    o_ref[0] = jnp.dot(cond, w_ref[0], preferred_element_type=F32,
                       precision=lax.Precision.HIGHEST) + b_ref[0]


def _ada_mods(c, w_ada, b_ada):
    B, D = c.shape
    n = w_ada.shape[0] * w_ada.shape[1]
    E = w_ada.shape[-1]
    rows = 8
    cp = jnp.zeros((rows, D), F32).at[:B].set(c.astype(F32))
    tn = 1024
    out = pl.pallas_call(
        _ada_kernel,
        grid=(n, E // tn),
        in_specs=[pl.BlockSpec((rows, D), lambda i, j: (0, 0)),
                  pl.BlockSpec((1, D, tn), lambda i, j: (i, 0, j)),
                  pl.BlockSpec((1, 1, tn), lambda i, j: (i, 0, j))],
        out_specs=pl.BlockSpec((1, rows, tn), lambda i, j: (i, 0, j)),
        out_shape=jax.ShapeDtypeStruct((n, rows, E), F32),
        compiler_params=_params(("parallel", "parallel")),
        name="ada_mods",
    )(cp, w_ada.reshape(n, D, E), b_ada.reshape(n, 1, E))
    return out[:, :B].reshape(w_ada.shape[0], w_ada.shape[1], B, E)


def _mod_proj_kernel(x_ref, shift_ref, scale_ref, w_ref, cs_ref, sn_ref, o_ref, *h_out, sections, h_mult):
    hf = x_ref[...] * (1.0 + scale_ref[0]) + shift_ref[0]
    h = hf.astype(BF16)
    if h_out:
        h_out[0][...] = (hf * h_mult).astype(BF16)
    tm = h.shape[0]
    reps = MXU_N // LANES
    for (c0, c1, rope, mult) in sections:
        for cc in range(c0, c1, MXU_N):
            acc = _dot(h, w_ref[:, cc:cc + MXU_N])
            if rope:
                cs = jnp.concatenate([cs_ref[...]] * reps, axis=1)
                sn = jnp.concatenate([sn_ref[...]] * reps, axis=1)
                first_half = (lax.broadcasted_iota(jnp.int32, (tm, MXU_N), 1) % HEAD_DIM) < HEAD_DIM // 2
                partner = jnp.where(first_half,
                                    pltpu.roll(acc, MXU_N - HEAD_DIM // 2, 1),
                                    pltpu.roll(acc, HEAD_DIM // 2, 1))
                acc = acc * cs + partner * sn
            if mult != 1.0:
                acc = acc * mult
            o_ref[:, cc:cc + MXU_N] = acc.astype(o_ref.dtype)


def _mod_proj(x, shift, scale, w, cs_tab, sn_tab, sections, S, emit_h=False, h_mult=1.0, tm=512):
    T, D = x.shape
    N = w.shape[1]
    tiles_per_seq = S // tm
    out_shape = [jax.ShapeDtypeStruct((T, N), BF16)]
    out_specs = [pl.BlockSpec((tm, N), lambda i: (i, 0))]
    if emit_h:
        out_shape.append(jax.ShapeDtypeStruct((T, D), BF16))
        out_specs.append(pl.BlockSpec((tm, D), lambda i: (i, 0)))
    res = pl.pallas_call(
        functools.partial(_mod_proj_kernel, sections=sections, h_mult=h_mult),
        grid=(T // tm,),
        in_specs=[pl.BlockSpec((tm, D), lambda i: (i, 0)),
                  pl.BlockSpec((1, 1, D), lambda i: (i // tiles_per_seq, 0, 0)),
                  pl.BlockSpec((1, 1, D), lambda i: (i // tiles_per_seq, 0, 0)),
                  pl.BlockSpec((D, N), lambda i: (0, 0)),
                  pl.BlockSpec((tm, LANES), lambda i: (i % tiles_per_seq, 0)),
                  pl.BlockSpec((tm, LANES), lambda i: (i % tiles_per_seq, 0))],
        out_specs=out_specs,
        out_shape=out_shape,
        compiler_params=_params(("parallel",)),
        name="mod_proj",
    )(x, shift, scale, w, cs_tab, sn_tab)
    return res if emit_h else res[0]


def _gated_ln(x, y, gate, g, b):
    z = DEEPNORM_ALPHA * x + gate * y
    mu = jnp.mean(z, axis=-1, keepdims=True)
    zc = z - mu
    var = jnp.mean(zc * zc, axis=-1, keepdims=True)
    return zc * lax.rsqrt(var + LN_EPS) * g + b


def _proj_ln_kernel(*refs, widths):
    n = len(widths)
    o_refs = refs[:n]
    w_ref, x_ref, gate_ref, g_ref, b_ref, out_ref = refs[n:]
    y = None
    k0 = 0
    for o_ref, wd in zip(o_refs, widths):
        part = _dot(o_ref[...], w_ref[k0:k0 + wd, :])
        y = part if y is None else y + part
        k0 += wd
    out_ref[...] = _gated_ln(x_ref[...], y, gate_ref[0], g_ref[...], b_ref[...])


def _proj_ln(o_list, w, x, gate, g, b, S, tm=512):
    T, D = x.shape
    widths = tuple(o.shape[1] for o in o_list)
    tiles_per_seq = S // tm
    in_specs = [pl.BlockSpec((tm, wd), lambda i: (i, 0)) for wd in widths]
    in_specs += [pl.BlockSpec((sum(widths), D), lambda i: (0, 0)),
                 pl.BlockSpec((tm, D), lambda i: (i, 0)),
                 pl.BlockSpec((1, 1, D), lambda i: (i // tiles_per_seq, 0, 0)),
                 pl.BlockSpec((1, D), lambda i: (0, 0)),
                 pl.BlockSpec((1, D), lambda i: (0, 0))]
    return pl.pallas_call(
        functools.partial(_proj_ln_kernel, widths=widths),
        grid=(T // tm,),
        in_specs=in_specs,
        out_specs=pl.BlockSpec((tm, D), lambda i: (i, 0)),
        out_shape=jax.ShapeDtypeStruct((T, D), F32),
        compiler_params=_params(("parallel",)),
        name="proj_ln",
    )(*o_list, w, x, gate, g.reshape(1, D), b.reshape(1, D))


def _na_kernel(q_ref, k_ref, v_ref, bias_ref, o_ref, *, grid_rows):
    blk = pl.program_id(2)
    start_row = jnp.clip(NA_QROWS * blk - NA_WIN_ROWS // 2, 0, grid_rows - NA_KROWS)
    start = pl.multiple_of(start_row * GRID_W, GRID_W)
    nk = NA_KROWS * GRID_W
    lo_q = _lo_lanes((q_ref.shape[0], PAIR))
    lo_v = _lo_lanes((nk, PAIR))
    for pr in range(q_ref.shape[1] // PAIR):
        cols = slice(pr * PAIR, (pr + 1) * PAIR)
        kw = k_ref[pl.ds(start, nk), cols]
        vw = v_ref[pl.ds(start, nk), cols]
        q = q_ref[:, cols]
        out = None
        for c in range(2):
            qm = jnp.where(lo_q if c == 0 else ~lo_q, q, jnp.zeros_like(q))
            s = _dot_nt(qm, kw) + bias_ref[0, 2 * pr + c]
            m = jnp.max(s, axis=1, keepdims=True)
            e = jnp.exp(s - m)
            p = (e / jnp.sum(e, axis=1, keepdims=True)).astype(BF16)
            vm = jnp.where(lo_v if c == 0 else ~lo_v, vw, jnp.zeros_like(vw))
            part = _dot(p, vm)
            out = part if out is None else out + part
        o_ref[:, cols] = out.astype(o_ref.dtype)


def _na_bias_tables(rpb_i, grid_rows):
    nblk = grid_rows // NA_QROWS
    ndr, ndc = 2 * NA_WIN_ROWS - 1, 2 * NA_WIN_COLS - 1
    col = np.arange(GRID_W)
    col_start = np.clip(col - NA_WIN_COLS // 2, 0, GRID_W - NA_WIN_COLS)
    col_ok = (col[None, :] >= col_start[:, None]) & (col[None, :] < col_start[:, None] + NA_WIN_COLS)
    dc = np.clip(col[None, :] - col[:, None] + NA_WIN_COLS - 1, 0, ndc - 1)
    pick = (dc[None] == np.arange(ndc)[:, None, None]).astype(np.float32)
    blocks = jnp.einsum('hrd,dqk->hrqk', rpb_i.astype(F32), jnp.asarray(pick), precision=lax.Precision.HIGHEST)
    blocks = jnp.where(jnp.asarray(col_ok)[None, None], blocks, NEG)
    blocks = jnp.concatenate([blocks, jnp.full((NA_HEADS, 1, GRID_W, GRID_W), NEG, F32)], axis=1)
    tabs = []
    for blk in (0, 1, nblk - 1):
        start_row = int(np.clip(NA_QROWS * blk - NA_WIN_ROWS // 2, 0, grid_rows - NA_KROWS))
        qr = NA_QROWS * blk + np.arange(NA_QROWS)
        kr = start_row + np.arange(NA_KROWS)
        row_start = np.clip(qr - NA_WIN_ROWS // 2, 0, grid_rows - NA_WIN_ROWS)
        row_ok = (kr[None, :] >= row_start[:, None]) & (kr[None, :] < row_start[:, None] + NA_WIN_ROWS)
        dr = np.where(row_ok, kr[None, :] - qr[:, None] + NA_WIN_ROWS - 1, ndr)
        tab = jnp.concatenate(
            [jnp.concatenate([blocks[:, int(dr[a, b])] for b in range(NA_KROWS)], axis=-1)
             for a in range(NA_QROWS)], axis=-2)
        tabs.append(tab)
    return jnp.stack(tabs)


def _na_attention(proj, rpb_i, B, S):
    T = B * S
    grid_rows = S // GRID_W
    nblk = grid_rows // NA_QROWS
    nq, nk = NA_QROWS * GRID_W, NA_KROWS * GRID_W
    npairs = A_WIDTH // PAIR
    bias = _na_bias_tables(rpb_i, grid_rows)

    def cls(j):
        return jnp.where(j == 0, 0, jnp.where(j == nblk - 1, 2, 1))

    ppb = NA_PAIRS_PER_STEP
    ngroups = npairs // ppb
    wd = ppb * PAIR
    return pl.pallas_call(
        functools.partial(_na_kernel, grid_rows=grid_rows),
        grid=(B, ngroups, nblk),
        in_specs=[pl.BlockSpec((nq, wd), lambda b, p, j: (b * nblk + j, p)),
                  pl.BlockSpec((S, wd), lambda b, p, j: (b, ngroups + p)),
                  pl.BlockSpec((S, wd), lambda b, p, j: (b, 2 * ngroups + p)),
                  pl.BlockSpec((1, 2 * ppb, nq, nk), lambda b, p, j: (cls(j), p, 0, 0))],
        out_specs=pl.BlockSpec((nq, wd), lambda b, p, j: (b * nblk + j, p)),
        out_shape=jax.ShapeDtypeStruct((T, A_WIDTH), BF16),
        compiler_params=_params(("parallel", "parallel", "arbitrary")),
        name="na_attention",
    )(proj, proj, proj, bias)


def _diff_kernel(q_ref, k_ref, v_ref, lamp_ref, subg_ref, o_ref, m_ref, l_ref, acc_ref, *, lambda_init):
    kv = pl.program_id(3)

    @pl.when(kv == 0)
    def _():
        m_ref[...] = jnp.full(m_ref.shape, -jnp.inf, F32)
        l_ref[...] = jnp.zeros(l_ref.shape, F32)
        acc_ref[...] = jnp.zeros(acc_ref.shape, F32)

    q = q_ref[...]
    k = k_ref[...]
    v = v_ref[...]
    lo = _lo_lanes(q.shape)
    reps = k.shape[0] // LANES
    for c in range(2):
        qm = jnp.where(lo if c == 0 else ~lo, q, jnp.zeros_like(q))
        s = _dot_nt(qm, k)
        m_prev = m_ref[c]
        m_new = jnp.maximum(m_prev, jnp.max(s, axis=1, keepdims=True))
        alpha = jnp.exp2(m_prev - m_new)
        p = jnp.exp2(s - jnp.concatenate([m_new] * reps, axis=1))
        l_ref[c] = alpha * l_ref[c] + jnp.sum(p, axis=1, keepdims=True)
        acc_ref[c] = alpha * acc_ref[c] + _dot(p.astype(BF16), v)
        m_ref[c] = m_new

    @pl.when(kv == pl.num_programs(3) - 1)
    def _():
        lp = lamp_ref[...]
        lam = (jnp.exp(jnp.sum(lp[0:1] * lp[1:2], axis=1, keepdims=True))
               - jnp.exp(jnp.sum(lp[2:3] * lp[3:4], axis=1, keepdims=True)) + lambda_init)
        o = acc_ref[0] / l_ref[0] - lam * (acc_ref[1] / l_ref[1])
        o = o * lax.rsqrt(jnp.mean(o * o, axis=-1, keepdims=True) + LN_EPS) * subg_ref[...]
        o_ref[...] = (o * (1.0 - lambda_init)).astype(o_ref.dtype)


def _diff_attention(proj, lam_params, sub_g, lambda_init, B, S, tq=1024, tk=2048):
    T = B * S
    off = 3 * A_WIDTH // PAIR
    nh = DIFF_HEADS
    nq, nkv = S // tq, S // tk
    return pl.pallas_call(
        functools.partial(_diff_kernel, lambda_init=lambda_init),
        grid=(B, nh, nq, nkv),
        in_specs=[pl.BlockSpec((tq, PAIR), lambda b, h, i, j: (b * nq + i, off + h)),
                  pl.BlockSpec((tk, PAIR), lambda b, h, i, j: (b * nkv + j, off + nh + h)),
                  pl.BlockSpec((tk, PAIR), lambda b, h, i, j: (b * nkv + j, off + 2 * nh + h)),
                  pl.BlockSpec((4, HEAD_DIM), lambda b, h, i, j: (0, 0)),
                  pl.BlockSpec((1, PAIR), lambda b, h, i, j: (0, 0))],
        out_specs=pl.BlockSpec((tq, PAIR), lambda b, h, i, j: (b * nq + i, h)),
        out_shape=jax.ShapeDtypeStruct((T, B_WIDTH), BF16),
        scratch_shapes=[pltpu.VMEM((2, tq, LANES), F32),
                        pltpu.VMEM((2, tq, LANES), F32),
                        pltpu.VMEM((2, tq, PAIR), F32)],
        compiler_params=_params(("parallel", "parallel", "parallel", "arbitrary")),
        name="diff_attention",
    )(proj, proj, proj, lam_params, sub_g.reshape(1, PAIR))


def _swa_kernel(sink_ref, q_ref, kp_ref, km_ref, kn_ref, vp_ref, vm_ref, vn_ref, o_ref, kbuf, vbuf, *, nblocks):
    kvh = pl.program_id(1)
    qt = pl.program_id(2)
    blk = SWA_BLOCK
    nsub = q_ref.shape[0] // blk
    group = SWA_Q_HEADS // SWA_KV_HEADS
    kbuf[0:blk] = kp_ref[...]
    kbuf[blk:blk + nsub * blk] = km_ref[...]
    kbuf[blk + nsub * blk:] = kn_ref[...]
    vbuf[0:blk] = vp_ref[...]
    vbuf[blk:blk + nsub * blk] = vm_ref[...]
    vbuf[blk + nsub * blk:] = vn_ref[...]

    qi = lax.broadcasted_iota(jnp.int32, (blk, 3 * blk), 0)
    kk = lax.broadcasted_iota(jnp.int32, (blk, 3 * blk), 1)
    band = (kk - qi >= blk - SWA_WINDOW) & (kk - qi <= blk + SWA_WINDOW)
    lo = _lo_lanes((blk, PAIR))
    for j in range(nsub):
        n = qt * nsub + j
        first_key = jnp.where(n > 0, 0, blk)
        end_key = jnp.where(n < nblocks - 1, 3 * blk, 2 * blk)
        ok = band & (kk >= first_key) & (kk < end_key)
        kw = kbuf[j * blk:(j + 3) * blk]
        vw = vbuf[j * blk:(j + 3) * blk]
        qs = []
        for pr in range(group // 2):
            qp = q_ref[j * blk:(j + 1) * blk, pr * PAIR:(pr + 1) * PAIR]
            qs.append(jnp.where(lo, qp, jnp.zeros_like(qp)))
            qs.append(jnp.where(lo, jnp.zeros_like(qp), qp))
        s_all = _dot_nt(jnp.concatenate(qs, axis=0), kw)
        ps = []
        for g in range(group):
            sink = sink_ref[kvh * group + g]
            s = jnp.where(ok, s_all[g * blk:(g + 1) * blk], NEG)
            m = jnp.maximum(jnp.max(s, axis=1, keepdims=True), sink)
            e = jnp.exp(s - m)
            den = jnp.sum(e, axis=1, keepdims=True) + jnp.exp(sink - m)
            ps.append((e / den).astype(BF16))
        o_all = _dot(jnp.concatenate(ps, axis=0), vw)
        for pr in range(group // 2):
            o_pair = jnp.where(lo, o_all[(2 * pr) * blk:(2 * pr + 1) * blk],
                               o_all[(2 * pr + 1) * blk:(2 * pr + 2) * blk])
            o_ref[j * blk:(j + 1) * blk, pr * PAIR:(pr + 1) * PAIR] = o_pair.astype(o_ref.dtype)


def _swa_attention(proj, sink_j, B, S, tq=1024):
    T = B * S
    blk = SWA_BLOCK
    nsub = tq // blk
    nqt = S // tq
    nblocks = S // blk
    group = SWA_Q_HEADS // SWA_KV_HEADS
    qw = group * HEAD_DIM
    koff = C_Q // PAIR
    voff = koff + SWA_KV_HEADS

    def prev_map(off):
        return lambda b, h, i: (b * nblocks + jnp.maximum(i * nsub - 1, 0), off + h)

    def main_map(off):
        return lambda b, h, i: (b * nqt + i, off + h)

    def next_map(off):
        return lambda b, h, i: (b * nblocks + jnp.minimum(i * nsub + nsub, nblocks - 1), off + h)

    return pl.pallas_call(
        functools.partial(_swa_kernel, nblocks=nblocks),
        grid=(B, SWA_KV_HEADS, nqt),
        in_specs=[pl.BlockSpec(memory_space=pltpu.SMEM),
                  pl.BlockSpec((tq, qw), lambda b, h, i: (b * nqt + i, h)),
                  pl.BlockSpec((blk, PAIR), prev_map(koff)),
                  pl.BlockSpec((tq, PAIR), main_map(koff)),
                  pl.BlockSpec((blk, PAIR), next_map(koff)),
                  pl.BlockSpec((blk, PAIR), prev_map(voff)),
                  pl.BlockSpec((tq, PAIR), main_map(voff)),
                  pl.BlockSpec((blk, PAIR), next_map(voff))],
        out_specs=pl.BlockSpec((tq, qw), lambda b, h, i: (b * nqt + i, h)),
        out_shape=jax.ShapeDtypeStruct((T, C_Q), BF16),
        scratch_shapes=[pltpu.VMEM((tq + 2 * blk, PAIR), BF16),
                        pltpu.VMEM((tq + 2 * blk, PAIR), BF16)],
        compiler_params=_params(("parallel", "parallel", "parallel")),
        name="swa_attention",
    )(sink_j.astype(F32), proj, proj, proj, proj, proj, proj, proj)


def _extract_topk(s, k, want_rank=True):
    nrows = s.shape[0]
    row = lax.broadcasted_iota(jnp.int32, s.shape, 0).astype(F32)
    rank = jnp.full(s.shape, float(k), F32) if want_rank else None
    vals, firsts = [], []
    for r in range(k):
        m = jnp.max(s, axis=0, keepdims=True)
        first = jnp.min(jnp.where(s == m, row, float(nrows)), axis=0, keepdims=True)
        hit = row == first
        if want_rank:
            rank = jnp.where(hit, float(r), rank)
        s = jnp.where(hit, -jnp.inf, s)
        vals.append(m)
        firsts.append(first)
    return rank, jnp.concatenate(vals, axis=0), jnp.concatenate(firsts, axis=0)


def _dup_bf16_words(x):
    hi = pltpu.bitcast(x.astype(BF16).astype(F32), jnp.int32)
    return hi | lax.shift_right_logical(hi, 16)


def _peer_route_kernel(q_ref, keys_ref, n_ref, ea_ref, rb_ref, eb_ref):
    K = PEER_TOPK
    half = q_ref.shape[1] // 2
    q = q_ref[...]
    sa = _dot_nt(keys_ref[0, 0], q[:, :half])
    sb = _dot_nt(keys_ref[0, 1], q[:, half:])
    _, va, rows_a = _extract_topk(sa, K, want_rank=False)
    rank_b, vb, _ = _extract_topk(sb, K)

    ninf = jnp.full((8, sa.shape[1]), -jnp.inf, F32)
    sub = lax.broadcasted_iota(jnp.int32, ninf.shape, 0)
    pieces = [va[0:1] + vb]
    for ki in range(1, 8):
        pieces.append(jnp.where(sub < K // (ki + 1), va[ki:ki + 1] + vb[0:8], ninf))
    pieces.append(va[8:16] + vb[0:1])
    cand = jnp.concatenate(pieces, axis=0)
    rank_c, vc, _ = _extract_topk(cand, K)
    sel = (rank_c < K).astype(F32)
    counts = [jnp.sum(sel[0:16], axis=0, keepdims=True)]
    for ki in range(1, 8):
        counts.append(jnp.sum(sel[8 + 8 * ki:16 + 8 * ki], axis=0, keepdims=True))
    counts.append(sel[72:80])
    nvec = jnp.concatenate(counts, axis=0)
    z = jnp.sum(jnp.exp(vc - vc[0:1]), axis=0, keepdims=True)

    row = lax.broadcasted_iota(jnp.int32, sa.shape, 0).astype(F32)
    n_of_i = jnp.zeros(sa.shape, F32)
    for ki in range(K):
        n_of_i = jnp.where(row == rows_a[ki:ki + 1], nvec[ki:ki + 1], n_of_i)
    n_ref[0] = _dup_bf16_words(n_of_i)
    ea_ref[0] = _dup_bf16_words(jnp.exp(sa - va[0:1]) * (GELU_IN_SCALE / z))
    rb_ref[0] = pltpu.bitcast(rank_b.astype(BF16), jnp.int32)
    eb_ref[0] = pltpu.bitcast(jnp.exp(sb - vb[0:1]).astype(BF16), jnp.int32)


def _peer_route(q, sub_keys, tt=1024):
    T = q.shape[0]
    H = PEER_HEADS
    kd = q.shape[1] // H
    shp = jax.ShapeDtypeStruct((H, PEER_N_KEYS, T), jnp.int32)
    shp16 = jax.ShapeDtypeStruct((H, PEER_N_KEYS // 2, T), jnp.int32)
    spec = pl.BlockSpec((1, PEER_N_KEYS, tt), lambda i, h: (h, 0, i))
    spec16 = pl.BlockSpec((1, PEER_N_KEYS // 2, tt), lambda i, h: (h, 0, i))
    return pl.pallas_call(
        _peer_route_kernel,
        grid=(T // tt, H),
        in_specs=[pl.BlockSpec((tt, kd), lambda i, h: (i, h)),
                  pl.BlockSpec((1, 2, PEER_N_KEYS, kd // 2), lambda i, h: (h, 0, 0, 0))],
        out_specs=[spec, spec, spec16, spec16],
        out_shape=[shp, shp, shp16, shp16],
        compiler_params=_params(("parallel", "parallel")),
        name="peer_route",
    )(q, sub_keys)


def _peer_dense_kernel(h_ref, u_ref, vt_ref, n_ref, ea_ref, rb_ref, eb_ref, x_ref, gate_ref, g_ref, b_ref,
                       out_ref, acc_ref, pre_ref, w_ref):
    et = pl.program_id(1)

    @pl.when(et == 0)
    def _():
        acc_ref[...] = jnp.zeros(acc_ref.shape, F32)

    nk = PEER_N_KEYS
    ni = u_ref.shape[0] // nk
    nlc = h_ref.shape[0] // LANES
    pack = 16
    assert ni == 8, "the per-i rows of one expert tile are read as one aligned 8-sublane block"
    i0 = pl.multiple_of(et * ni, ni)
    half = nk
    igroup = 2

    def bcast_rows(blk, ii):
        row = pltpu.bitcast(jnp.broadcast_to(blk[ii:ii + 1], (pack // 2, LANES)), BF16)
        return jnp.concatenate([row] * (half // pack), axis=0)

    pre_ref[...] = _dot_nt(u_ref[...], h_ref[...])

    def gate_block(it, carry):
        lanes = pl.ds(pl.multiple_of((it // (nk // half)) * LANES, LANES), LANES)
        jh = it % (nk // half)
        jwords = pl.ds(pl.multiple_of(jh * (half // 2), half // 2), half // 2)
        for ig in range(0, ni, igroup):
            gates = [None] * igroup
            for h in range(PEER_HEADS):
                n_blk = n_ref[h, pl.ds(i0, ni), lanes]
                ea_blk = ea_ref[h, pl.ds(i0, ni), lanes]
                rb = pltpu.bitcast(rb_ref[h, jwords, lanes], BF16)
                eb = pltpu.bitcast(eb_ref[h, jwords, lanes], BF16)
                for k in range(igroup):
                    part = jnp.where(rb < bcast_rows(n_blk, ig + k), eb * bcast_rows(ea_blk, ig + k), 0)
                    gates[k] = part if gates[k] is None else gates[k] + part
            for k in range(igroup):
                rows = pl.ds(pl.multiple_of((ig + k) * nk + jh * half, half), half)
                pre = pre_ref[rows, lanes]
                act = pre * (1.0 + lax.erf(pre))
                w_ref[rows, lanes] = gates[k] * act.astype(BF16)
        return carry

    lax.fori_loop(0, (nk // half) * nlc, gate_block, 0, unroll=8)
    acc_ref[...] += _dot(vt_ref[...], w_ref[...])

    @pl.when(et == pl.num_programs(1) - 1)
    def _():
        y = acc_ref[...].T
        out_ref[...] = _gated_ln(x_ref[...], y, gate_ref[0], g_ref[...], b_ref[...])


def _peer_dense(h, u, vt, routing, x, gate, g, b, S, tt=512, te=1024):
    T, D = x.shape
    E = u.shape[0]
    H, nk = PEER_HEADS, PEER_N_KEYS
    tiles_per_seq = S // tt
    rspec = pl.BlockSpec((H, nk, tt), lambda i, e: (0, 0, i))
    rspec16 = pl.BlockSpec((H, nk // 2, tt), lambda i, e: (0, 0, i))
    return pl.pallas_call(
        _peer_dense_kernel,
        grid=(T // tt, E // te),
        in_specs=[pl.BlockSpec((tt, D), lambda i, e: (i, 0)),
                  pl.BlockSpec((te, D), lambda i, e: (e, 0)),
                  pl.BlockSpec((D, te), lambda i, e: (0, e)),
                  rspec, rspec, rspec16, rspec16,
                  pl.BlockSpec((tt, D), lambda i, e: (i, 0)),
                  pl.BlockSpec((1, 1, D), lambda i, e: (i // tiles_per_seq, 0, 0)),
                  pl.BlockSpec((1, D), lambda i, e: (0, 0)),
                  pl.BlockSpec((1, D), lambda i, e: (0, 0))],
        out_specs=pl.BlockSpec((tt, D), lambda i, e: (i, 0)),
        out_shape=jax.ShapeDtypeStruct((T, D), F32),
        scratch_shapes=[pltpu.VMEM((D, tt), F32), pltpu.VMEM((te, tt), F32), pltpu.VMEM((te, tt), BF16)],
        compiler_params=_params(("parallel", "arbitrary")),
        name="peer_dense",
    )(h, u, vt, *routing, x, gate, g.reshape(1, D), b.reshape(1, D))


def _rope_tables(S):
    inv = 1.0 / (ROPE_THETA ** (jnp.arange(0, HEAD_DIM, 2, dtype=F32) / HEAD_DIM))
    ang = jnp.arange(S, dtype=F32)[:, None] * inv[None, :]
    cos, sin = jnp.cos(ang), jnp.sin(ang)
    reps = LANES // HEAD_DIM
    cs = jnp.concatenate([cos, cos] * reps, axis=1)
    sn = jnp.concatenate([-sin, sin] * reps, axis=1)
    return cs, sn


def _split3(m):
    B = m.shape[0]
    D = m.shape[1] // 3
    return (m[:, :D].reshape(B, 1, D), m[:, D:2 * D].reshape(B, 1, D), m[:, 2 * D:].reshape(B, 1, D))


def kernel(x, c, w_ada, b_ada, ln_g, ln_b, w_in_even, rpb, lam_q1, lam_k1, lam_q2, lam_k2, diff_sub_g,
           w_out_even, w_in_odd, sink, w_out_odd, peer_w_query, peer_sub_keys, peer_u, peer_v):
    B, S, D = x.shape
    T = B * S
    qscale = HEAD_DIM ** -0.5
    cs_tab, sn_tab = _rope_tables(S)
    mods = _ada_mods(c, w_ada, b_ada)
    xf = x.reshape(T, D).astype(F32)

    even_sections = ((0, A_WIDTH, False, qscale),
                     (A_WIDTH, 3 * A_WIDTH, False, 1.0),
                     (3 * A_WIDTH, 3 * A_WIDTH + B_WIDTH, True, qscale * math.log2(math.e)),
                     (3 * A_WIDTH + B_WIDTH, 3 * A_WIDTH + 2 * B_WIDTH, True, 1.0),
                     (3 * A_WIDTH + 2 * B_WIDTH, EVEN_IN, False, 1.0))
    dup = np.repeat(np.arange(SWA_KV_HEADS), 2)[:, None] * HEAD_DIM + np.arange(HEAD_DIM)[None, :]
    odd_cols = np.concatenate([np.arange(C_Q), C_Q + dup.reshape(-1), C_Q + C_KV + dup.reshape(-1)])
    odd_sections = ((0, C_Q, True, qscale),
                    (C_Q, C_Q + 2 * C_KV, True, 1.0),
                    (C_Q + 2 * C_KV, C_Q + 4 * C_KV, False, 1.0))
    peer_sections = ((0, peer_w_query.shape[-1], False, 1.0),)

    for layer in range(DEPTH):
        shift, scale, gate = _split3(mods[layer, 0])
        if layer % 2 == 0:
            i = layer // 2
            proj = _mod_proj(xf, shift, scale, w_in_even[i].astype(BF16), cs_tab, sn_tab, even_sections, S)
            o_a = _na_attention(proj, rpb[i], B, S)
            lambda_init = 0.8 - 0.6 * math.exp(-0.3 * layer)
            lam_params = jnp.stack([lam_q1[i], lam_k1[i], lam_q2[i], lam_k2[i]]).astype(F32)
            o_b = _diff_attention(proj, lam_params, diff_sub_g[i].astype(F32), lambda_init, B, S)
            xf = _proj_ln([o_a, o_b], w_out_even[i].astype(BF16), xf, gate, ln_g[layer, 0], ln_b[layer, 0], S)
        else:
            j = layer // 2
            proj = _mod_proj(xf, shift, scale, w_in_odd[j][:, odd_cols].astype(BF16), cs_tab, sn_tab,
                             odd_sections, S)
            o_c = _swa_attention(proj, sink[j], B, S)
            xf = _proj_ln([o_c], w_out_odd[j].astype(BF16), xf, gate, ln_g[layer, 0], ln_b[layer, 0], S)
        shift2, scale2, gate2 = _split3(mods[layer, 1])
        q, h2 = _mod_proj(xf, shift2, scale2, peer_w_query[layer].astype(BF16), cs_tab, sn_tab,
                          peer_sections, S, emit_h=True, h_mult=GELU_IN_SCALE)
        routing = _peer_route(q, peer_sub_keys[layer].astype(BF16))
        xf = _peer_dense(h2, peer_u[layer].astype(BF16), peer_v[layer].T.astype(BF16), routing,
                         xf, gate2, ln_g[layer, 1], ln_b[layer, 1], S)
    return xf.reshape(B, S, D).astype(x.dtype)
```
